```python
import math
import jax, jax.numpy as jnp
from jax import lax
import numpy as np

D_MODEL = 2048
BATCH = 2
SEQ = 8192
DEPTH = 2

HEAD_DIM = 128
MIX_WIDTH = D_MODEL
A_HEADS = D_MODEL // 256
A_QK_DIM = 64
A_V_DIM = 128
B_HEADS = D_MODEL // 256
MOBA_BLOCK = 256
MOBA_TOPK = 3
MOBA_Q_CHUNK = 64
C_HEADS = D_MODEL // HEAD_DIM
Q_BLOCK = 128
N_EXPERTS = 16
N_GROUPS = 4
EXPERTS_PER_GROUP = N_EXPERTS // N_GROUPS
TOP_K = 2
D_FF_EXPERT = 512

EPS = 1e-6
NEG_INF = -1e30
N_EVEN = (DEPTH + 1) // 2
N_ODD = DEPTH // 2
EVEN_SPLITS = [A_HEADS * 2 * A_QK_DIM, A_HEADS * 2 * A_QK_DIM, A_HEADS * A_V_DIM,
               B_HEADS * HEAD_DIM, B_HEADS * HEAD_DIM, B_HEADS * HEAD_DIM]
EVEN_IN = sum(EVEN_SPLITS)
ODD_SPLITS = [C_HEADS * HEAD_DIM, C_HEADS * HEAD_DIM, C_HEADS * HEAD_DIM, C_HEADS]
ODD_IN = sum(ODD_SPLITS)

kernel_name = 'hybrid_diff_moba_fox_grouped_moe_adaln'


def alibi_slopes(n):
    return jnp.asarray(2.0 ** (-8.0 * np.arange(1, n + 1) / n), dtype=jnp.float32)


def rms_norm(x, g):
    xf = x.astype(jnp.float32)
    y = xf * lax.rsqrt(jnp.mean(xf * xf, axis=-1, keepdims=True) + EPS)
    return y.astype(x.dtype) * g


def modulate(x, g, shift, scale):
    return rms_norm(x, g) * (1.0 + scale[:, None, :]) + shift[:, None, :]


def unblock(out):
    n, b, h, qb, d = out.shape
    return out.transpose(1, 0, 3, 2, 4).reshape(b, n * qb, h, d)


def split_at(t, sizes):
    return jnp.split(t, list(np.cumsum(sizes)[:-1]), axis=-1)


def diff_attention(q, k, v, lam, slopes):
    S = q.shape[2]
    scale = A_QK_DIM ** -0.5
    kpos = jnp.arange(S)

    def block(i):
        q_i = lax.dynamic_slice_in_dim(q, i * Q_BLOCK, Q_BLOCK, axis=2)
        logits = jnp.einsum('bhqmd,bhkmd->bhmqk', q_i, k).astype(jnp.float32) * scale
        dist = (i * Q_BLOCK + jnp.arange(Q_BLOCK))[:, None] - kpos[None, :]
        bias = -slopes[:, None, None] * dist.astype(jnp.float32)
        logits = jnp.where(dist >= 0, logits + bias[None, :, None], NEG_INF)
        p = jax.nn.softmax(logits, axis=-1)
        attn = p[:, :, 0] - lam * p[:, :, 1]
        return jnp.einsum('bhqk,bhkd->bhqd', attn.astype(v.dtype), v)

    return unblock(lax.map(block, jnp.arange(S // Q_BLOCK)))


def moba_attention(q, k, v, slopes):
    B, H, S, d = q.shape
    scale = d ** -0.5
    s_pad = -(-S // MOBA_BLOCK) * MOBA_BLOCK
    pad = ((0, 0), (0, 0), (0, s_pad - S), (0, 0))
    k_pad = jnp.pad(k, pad)
    v_pad = jnp.pad(v, pad)
    nb = s_pad // MOBA_BLOCK
    topk = min(MOBA_TOPK, nb)
    k_blk = k_pad.reshape(B, H, nb, MOBA_BLOCK, d)
    v_blk = v_pad.reshape(B, H, nb, MOBA_BLOCK, d)
    k_mean = k_blk.astype(jnp.float32).mean(axis=3).astype(k.dtype)
    bi = jnp.arange(B)[:, None, None, None]
    hi = jnp.arange(H)[None, :, None, None]
    blk_pos = jnp.arange(MOBA_BLOCK)

    def chunk(ci):
        start = ci * MOBA_Q_CHUNK
        own = start // MOBA_BLOCK
        qpos = start + jnp.arange(MOBA_Q_CHUNK)
        q_c = lax.dynamic_slice_in_dim(q, start, MOBA_Q_CHUNK, axis=2)
        gate = jnp.einsum('bhqd,bhnd->bhqn', q_c, k_mean).astype(jnp.float32)
        gate = jnp.where(jnp.arange(nb) < own, gate, NEG_INF)
        _, idx = lax.top_k(gate, topk)
        valid = idx < own
        k_sel = k_blk[bi, hi, idx]
        v_sel = v_blk[bi, hi, idx]
        s_sel = jnp.einsum('bhqd,bhqjpd->bhqjp', q_c, k_sel).astype(jnp.float32) * scale
        dist_sel = (qpos[None, None, :, None, None] - (idx[..., None] * MOBA_BLOCK + blk_pos)).astype(jnp.float32)
        s_sel = jnp.where(valid[..., None], s_sel - slopes[None, :, None, None, None] * dist_sel, NEG_INF)
        k_own = lax.dynamic_slice_in_dim(k_pad, own * MOBA_BLOCK, MOBA_BLOCK, axis=2)
        v_own = lax.dynamic_slice_in_dim(v_pad, own * MOBA_BLOCK, MOBA_BLOCK, axis=2)
        s_own = jnp.einsum('bhqd,bhpd->bhqp', q_c, k_own).astype(jnp.float32) * scale
        dist_own = qpos[:, None] - (own * MOBA_BLOCK + blk_pos)[None, :]
        s_own = jnp.where(dist_own >= 0, s_own - slopes[None, :, None, None] * dist_own.astype(jnp.float32), NEG_INF)
        logits = jnp.concatenate([s_sel.reshape(B, H, MOBA_Q_CHUNK, topk * MOBA_BLOCK), s_own], axis=-1)
        p = jax.nn.softmax(logits, axis=-1).astype(v.dtype)
        p_sel = p[..., :topk * MOBA_BLOCK].reshape(B, H, MOBA_Q_CHUNK, topk, MOBA_BLOCK)
        p_own = p[..., topk * MOBA_BLOCK:]
        return (jnp.einsum('bhqjp,bhqjpd->bhqd', p_sel, v_sel)
                + jnp.einsum('bhqp,bhpd->bhqd', p_own, v_own))

    return unblock(lax.map(chunk, jnp.arange(S // MOBA_Q_CHUNK)))


def forgetting_attention(q, k, v, log_f):
    S = q.shape[2]
    scale = HEAD_DIM ** -0.5
    cum = jnp.cumsum(log_f, axis=-1)
    kpos = jnp.arange(S)

    def block(i):
        q_i = lax.dynamic_slice_in_dim(q, i * Q_BLOCK, Q_BLOCK, axis=2)
        cum_i = lax.dynamic_slice_in_dim(cum, i * Q_BLOCK, Q_BLOCK, axis=2)
        logits = (jnp.einsum('bhqd,bhkd->bhqk', q_i, k).astype(jnp.float32) * scale
                  + cum_i[..., :, None] - cum[..., None, :])
        causal = (i * Q_BLOCK + jnp.arange(Q_BLOCK))[:, None] >= kpos[None, :]
        p = jax.nn.softmax(jnp.where(causal, logits, NEG_INF), axis=-1)
        return jnp.einsum('bhqk,bhkd->bhqd', p.astype(v.dtype), v)

    return unblock(lax.map(block, jnp.arange(S // Q_BLOCK)))


def even_mixer(h, w_in, w_out, lam_q1, lam_k1, lam_q2, lam_k2, subln_g, layer_idx):
    B, S, _ = h.shape
    a_q, a_k, a_v, b_q, b_k, b_v = split_at(h @ w_in, EVEN_SPLITS)
    a_q = a_q.reshape(B, S, A_HEADS, 2, A_QK_DIM).transpose(0, 2, 1, 3, 4)
    a_k = a_k.reshape(B, S, A_HEADS, 2, A_QK_DIM).transpose(0, 2, 1, 3, 4)
    a_v = a_v.reshape(B, S, A_HEADS, A_V_DIM).transpose(0, 2, 1, 3)
    heads = lambda t: t.reshape(B, S, B_HEADS, HEAD_DIM).transpose(0, 2, 1, 3)
    b_q, b_k, b_v = heads(b_q), heads(b_k), heads(b_v)
    slopes = alibi_slopes(A_HEADS + B_HEADS)
    lam_init = 0.8 - 0.6 * math.exp(-0.3 * layer_idx)
    f32 = jnp.float32
    lam = (jnp.exp(jnp.sum(lam_q1.astype(f32) * lam_k1.astype(f32)))
           - jnp.exp(jnp.sum(lam_q2.astype(f32) * lam_k2.astype(f32))) + lam_init)
    a_out = diff_attention(a_q, a_k, a_v, lam, slopes[0::2])
    a_out = rms_norm(a_out, subln_g) * (1.0 - lam_init)
    b_out = moba_attention(b_q, b_k, b_v, slopes[1::2])
    merged = jnp.concatenate([a_out.reshape(B, S, -1), b_out.reshape(B, S, -1)], axis=-1)
    return merged @ w_out


def odd_mixer(h, w_in, b_forget, w_out):
    B, S, _ = h.shape
    q, k, v, f_logit = split_at(h @ w_in, ODD_SPLITS)
    heads = lambda t: t.reshape(B, S, C_HEADS, HEAD_DIM).transpose(0, 2, 1, 3)
    log_f = jax.nn.log_sigmoid(f_logit.astype(jnp.float32) + b_forget.astype(jnp.float32)).transpose(0, 2, 1)
    out = forgetting_attention(heads(q), heads(k), heads(v), log_f)
    return out.reshape(B, S, -1) @ w_out


def moe_ffn(h, router_w, router_b, w1, w3, w2):
    B, S, D = h.shape
    t = h.reshape(-1, D)
    affinity = jax.nn.sigmoid((t @ router_w).astype(jnp.float32))
    sel = affinity + router_b.astype(jnp.float32)
    grp_score = lax.top_k(sel.reshape(-1, N_GROUPS, EXPERTS_PER_GROUP), TOP_K)[0].sum(-1)
    best = jnp.argmax(grp_score, axis=-1)
    in_grp = (jnp.arange(N_EXPERTS) // EXPERTS_PER_GROUP)[None, :] == best[:, None]
    _, idx = lax.top_k(jnp.where(in_grp, sel, NEG_INF), TOP_K)
    w = jnp.take_along_axis(affinity, idx, axis=-1)
    w = w / jnp.sum(w, axis=-1, keepdims=True)
    gates = jnp.sum(jax.nn.one_hot(idx, N_EXPERTS, dtype=jnp.float32) * w[..., None], axis=1)
    act = (jax.nn.silu(jnp.einsum('nd,edf->nef', t, w1)) * jnp.einsum('nd,edf->nef', t, w3)
           * gates[..., None].astype(t.dtype))
    return jnp.einsum('nef,efd->nd', act, w2).reshape(B, S, D)


def setup_inputs(seed: int = 0) -> dict:
    key = jax.random.key(seed)
    ks = jax.random.split(key, 24)
    D = D_MODEL

    def nrm(k, shape, scale):
        return jax.random.normal(k, shape, jnp.float32) * scale

    return {
        'x': nrm(ks[0], (BATCH, SEQ, D), 1.0),
        'c': nrm(ks[1], (BATCH, D), 1.0),
        'ada_w': nrm(ks[2], (DEPTH, D, 6 * D), 0.5 * D ** -0.5),
        'ada_b': nrm(ks[3], (DEPTH, 6 * D), 0.02),
        'attn_norm_g': 1.0 + nrm(ks[4], (DEPTH, D), 0.05),
        'ffn_norm_g': 1.0 + nrm(ks[5], (DEPTH, D), 0.05),
        'w_in_even': nrm(ks[6], (N_EVEN, D, EVEN_IN), D ** -0.5),
        'w_out_even': nrm(ks[7], (N_EVEN, MIX_WIDTH, D), MIX_WIDTH ** -0.5),
        'lam_q1': nrm(ks[8], (N_EVEN, A_QK_DIM), 0.1),
        'lam_k1': nrm(ks[9], (N_EVEN, A_QK_DIM), 0.1),
        'lam_q2': nrm(ks[10], (N_EVEN, A_QK_DIM), 0.1),
        'lam_k2': nrm(ks[11], (N_EVEN, A_QK_DIM), 0.1),
        'subln_g': 1.0 + nrm(ks[12], (N_EVEN, A_V_DIM), 0.05),
        'w_in_odd': nrm(ks[13], (N_ODD, D, ODD_IN), D ** -0.5),
        'b_forget': jax.random.uniform(ks[14], (N_ODD, C_HEADS), jnp.float32, 1.0, 4.0),
        'w_out_odd': nrm(ks[15], (N_ODD, MIX_WIDTH, D), MIX_WIDTH ** -0.5),
        'router_w': nrm(ks[16], (D, N_EXPERTS), D ** -0.5),
        'router_b': nrm(ks[17], (N_EXPERTS,), 0.01),
        'moe_w1': nrm(ks[18], (DEPTH, N_EXPERTS, D, D_FF_EXPERT), D ** -0.5),
        'moe_w3': nrm(ks[19], (DEPTH, N_EXPERTS, D, D_FF_EXPERT), D ** -0.5),
        'moe_w2': nrm(ks[20], (DEPTH, N_EXPERTS, D_FF_EXPERT, D), D_FF_EXPERT ** -0.5),
        'final_norm_g': 1.0 + nrm(ks[21], (D,), 0.05),
    }


def reference(x, c, ada_w, ada_b, attn_norm_g, ffn_norm_g, w_in_even, w_out_even,
              lam_q1, lam_k1, lam_q2, lam_k2, subln_g, w_in_odd, b_forget, w_out_odd,
              router_w, router_b, moe_w1, moe_w3, moe_w2, final_norm_g):
    cond = jax.nn.silu(c)
    for l in range(DEPTH):
        mod = cond @ ada_w[l] + ada_b[l]
        sh1, sc1, g1, sh2, sc2, g2 = jnp.split(mod, 6, axis=-1)
        h = modulate(x, attn_norm_g[l], sh1, sc1)
        j = l // 2
        if l % 2 == 0:
            y = even_mixer(h, w_in_even[j], w_out_even[j], lam_q1[j], lam_k1[j],
                           lam_q2[j], lam_k2[j], subln_g[j], l)
        else:
            y = odd_mixer(h, w_in_odd[j], b_forget[j], w_out_odd[j])
        x = x + g1[:, None, :] * y
        h = modulate(x, ffn_norm_g[l], sh2, sc2)
        x = x + g2[:, None, :] * moe_ffn(h, router_w, router_b, moe_w1[l], moe_w3[l], moe_w2[l])
    return rms_norm(x, final_norm_g)
```

```python
import functools
import math

import jax
import jax.numpy as jnp
import numpy as np
from jax import lax
from jax.experimental import pallas as pl
from jax.experimental.pallas import tpu as pltpu

F32 = jnp.float32
BF16 = jnp.bfloat16
HIGHEST = lax.Precision.HIGHEST

HEAD_DIM = 128
A_HEADS = 8
A_QK_DIM = 64
B_HEADS = 8
C_HEADS = 16
MOBA_BLOCK = 256
MOBA_TOPK = 3
N_EXPERTS = 16
N_GROUPS = 4
EXPERTS_PER_GROUP = N_EXPERTS // N_GROUPS
EPS = 1e-6
NEG_INF = -1e30
LOG2E = 1.4426950408889634

LANES = 128
SUBLANES = 8
VMEM_LIMIT_BYTES = 56 * 1024 * 1024

NT_DIMS = (((1,), (1,)), ((), ()))


def _compiler_params(semantics):
    return pltpu.CompilerParams(dimension_semantics=semantics, vmem_limit_bytes=VMEM_LIMIT_BYTES)


def _adaln_kernel(c_ref, w_ref, b_ref, o_ref):
    c = c_ref[...]
    cond = c * jax.nn.sigmoid(c)
    o_ref[0] = jnp.dot(cond, w_ref[0], precision=HIGHEST, preferred_element_type=F32) + b_ref[0]


def _adaln(c, ada_w, ada_b, *, tn=1024):
    depth, d, n = ada_w.shape
    bsz = c.shape[0]
    rows = -(-bsz // SUBLANES) * SUBLANES
    c_pad = jnp.pad(c, ((0, rows - bsz), (0, 0)))
    out = pl.pallas_call(
        _adaln_kernel,
        out_shape=jax.ShapeDtypeStruct((depth, rows, n), F32),
        grid=(depth, n // tn),
        in_specs=[
            pl.BlockSpec((rows, d), lambda l, j: (0, 0)),
            pl.BlockSpec((1, d, tn), lambda l, j: (l, 0, j)),
            pl.BlockSpec((1, 1, tn), lambda l, j: (l, 0, j)),
        ],
        out_specs=pl.BlockSpec((1, rows, tn), lambda l, j: (l, 0, j)),
        compiler_params=_compiler_params(("arbitrary", "arbitrary")),
        name="adaln",
    )(c_pad, ada_w, ada_b.reshape(depth, 1, n))
    return out[:, :bsz].reshape(depth, bsz, 6, d)


def _modulated_norm(x, gain_row, mod_ref, shift_idx, scale_idx):
    ms = jnp.mean(x * x, axis=-1, keepdims=True)
    xn = x * lax.rsqrt(ms + EPS)
    gain = gain_row * (1.0 + mod_ref[0, scale_idx:scale_idx + 1, :])
    return xn * gain + mod_ref[0, shift_idx:shift_idx + 1, :]


def _inproj_kernel(*refs, with_forget):
    if with_forget:
        x_ref, g_ref, mod_ref, w_ref, cs_ref, wf_ref, o_ref, z_ref, h_sc = refs
    else:
        x_ref, g_ref, mod_ref, w_ref, cs_ref, o_ref, h_sc = refs

    @pl.when(pl.program_id(1) == 0)
    def _():
        h = _modulated_norm(x_ref[...], g_ref[...], mod_ref, 0, 1)
        h_sc[...] = h.astype(BF16)
        if with_forget:
            z_ref[...] = jnp.dot(h, wf_ref[...], precision=HIGHEST, preferred_element_type=F32)

    acc = jnp.dot(h_sc[...], w_ref[...], preferred_element_type=F32)
    o_ref[...] = (acc * cs_ref[...]).astype(o_ref.dtype)


def _inproj(x2, gain, mod, w_bf16, col_scale, wf=None, *, seq, tm=1024, tn=1024):
    n_tok, d = x2.shape
    n_out = w_bf16.shape[1]
    tiles_per_seq = seq // tm
    with_forget = wf is not None
    in_specs = [
        pl.BlockSpec((tm, d), lambda i, j: (i, 0)),
        pl.BlockSpec((1, d), lambda i, j: (0, 0)),
        pl.BlockSpec((1, 6, d), lambda i, j: (i // tiles_per_seq, 0, 0)),
        pl.BlockSpec((d, tn), lambda i, j: (0, j)),
        pl.BlockSpec((1, tn), lambda i, j: (0, j)),
    ]
    args = [x2, gain.reshape(1, d), mod, w_bf16, col_scale]
    out_shape = [jax.ShapeDtypeStruct((n_tok, n_out), BF16)]
    out_specs = [pl.BlockSpec((tm, tn), lambda i, j: (i, j))]
    if with_forget:
        in_specs.append(pl.BlockSpec((d, LANES), lambda i, j: (0, 0)))
        args.append(wf)
        out_shape.append(jax.ShapeDtypeStruct((n_tok, LANES), F32))
        out_specs.append(pl.BlockSpec((tm, LANES), lambda i, j: (i, 0)))
    return pl.pallas_call(
        functools.partial(_inproj_kernel, with_forget=with_forget),
        out_shape=out_shape,
        grid=(n_tok // tm, n_out // tn),
        in_specs=in_specs,
        out_specs=out_specs,
        scratch_shapes=[pltpu.VMEM((tm, d), BF16)],
        compiler_params=_compiler_params(("arbitrary", "arbitrary")),
        name="inproj_forget" if with_forget else "inproj",
    )(*args)


def _fcum_kernel(z_ref, bf_ref, o_ref, carry_sc):
    @pl.when(pl.program_id(1) == 0)
    def _():
        carry_sc[...] = jnp.zeros_like(carry_sc)

    z = z_ref[0] + bf_ref[...]
    log_f = jnp.minimum(z, 0.0) - jnp.log1p(jnp.exp(-jnp.abs(z)))
    tc = z.shape[0]
    row = lax.broadcasted_iota(jnp.int32, (tc, tc), 0)
    col = lax.broadcasted_iota(jnp.int32, (tc, tc), 1)
    tri = jnp.where(row >= col, 1.0, 0.0).astype(F32)
    cum = jnp.dot(tri, log_f, precision=HIGHEST, preferred_element_type=F32) + carry_sc[...]
    carry_sc[...] = cum[tc - 1:tc, :]
    o_ref[0] = cum.T


def _forget_cumsum(z, b_forget_row, *, tc=256):
    bsz, seq, _ = z.shape
    return pl.pallas_call(
        _fcum_kernel,
        out_shape=jax.ShapeDtypeStruct((bsz, LANES, seq), F32),
        grid=(bsz, seq // tc),
        in_specs=[
            pl.BlockSpec((1, tc, LANES), lambda b, t: (b, t, 0)),
            pl.BlockSpec((1, LANES), lambda b, t: (0, 0)),
        ],
        out_specs=pl.BlockSpec((1, LANES, tc), lambda b, t: (b, 0, t)),
        scratch_shapes=[pltpu.VMEM((1, LANES), F32)],
        compiler_params=_compiler_params(("arbitrary", "arbitrary")),
        name="forget_cumsum",
    )(z, b_forget_row)


def _attn_kernel(*refs, mode, tq, tk, seq, lam_init):
    if mode == "diff":
        slopes_ref, q_ref, k_ref, v_ref, lam_ref, subg_ref, o_ref, m_sc, l_sc, acc_sc = refs
    elif mode == "moba":
        slopes_ref, q_ref, k_ref, v_ref, o_ref, m_sc, l_sc, acc_sc, kmean_sc, sel_sc = refs
    else:
        q_ref, k_ref, v_ref, cum_ref, o_ref, m_sc, l_sc, acc_sc = refs

    head = pl.program_id(1)
    qi = pl.program_id(2)
    q0 = qi * tq
    rows = 2 * tq if mode == "diff" else tq
    reps = tk // LANES

    q = q_ref[0]
    if mode == "diff":
        lane = lax.broadcasted_iota(jnp.int32, (tq, HEAD_DIM), 1)
        zero = jnp.zeros_like(q)
        q = jnp.concatenate([jnp.where(lane < A_QK_DIM, q, zero),
                             jnp.where(lane >= A_QK_DIM, q, zero)], axis=0)

    def col_bias(j):
        if mode == "fox":
            return cum_ref[0, 0, pl.ds(j, 1), :] * (-LOG2E)
        kpos = lax.broadcasted_iota(jnp.int32, (1, tk), 1) + (j * tk - q0)
        return kpos.astype(F32) * slopes_ref[head]

    def scores(j):
        start = pl.multiple_of(j * tk, tk)
        k = k_ref[0, pl.ds(start, tk), :]
        s = lax.dot_general(q, k, NT_DIMS, preferred_element_type=F32)
        return s + col_bias(j), start

    if mode == "moba":
        n_blocks = seq // MOBA_BLOCK

        @pl.when(qi == 0)
        def _():
            kmean_sc[...] = jnp.zeros_like(kmean_sc)
            for n in range(n_blocks):
                blk = k_ref[0, n * MOBA_BLOCK:(n + 1) * MOBA_BLOCK, :].astype(F32)
                kmean_sc[n:n + 1, :] = jnp.mean(blk, axis=0, keepdims=True)

        gate = lax.dot_general(q.astype(F32), kmean_sc[...], NT_DIMS, precision=HIGHEST,
                               preferred_element_type=F32)
        blk_id = lax.broadcasted_iota(jnp.int32, (tq, LANES), 1).astype(F32)
        own = qi.astype(F32)
        gate = jnp.where(blk_id < own, gate, NEG_INF)
        for r in range(MOBA_TOPK):
            best = jnp.max(gate, axis=1, keepdims=True)
            pick = jnp.min(jnp.where(gate == best, blk_id, float(LANES)), axis=1, keepdims=True)
            sel_sc[r] = jnp.broadcast_to(jnp.where(pick < own, pick, -1.0), (tq, LANES))
            gate = jnp.where(blk_id == pick, -jnp.inf, gate)

    jd = q0 // tk
    s, start = scores(jd)
    rpos = lax.broadcasted_iota(jnp.int32, (rows, tk), 0)
    if mode == "diff":
        rpos = jnp.where(rpos >= tq, rpos - tq, rpos)
    cpos = lax.broadcasted_iota(jnp.int32, (rows, tk), 1) + (jd * tk - q0)
    s = jnp.where(cpos <= rpos, s, NEG_INF)
    m0 = jnp.max(s, axis=1, keepdims=True)
    p = jnp.exp2(s - m0)
    m_sc[...] = jnp.broadcast_to(m0, (rows, LANES))
    l_sc[...] = jnp.broadcast_to(jnp.sum(p, axis=1, keepdims=True), (rows, LANES))
    acc_sc[...] = jnp.dot(p.astype(BF16), v_ref[0, pl.ds(start, tk), :],
                          preferred_element_type=F32)

    def past_tile(j, carry):
        s, start = scores(j)
        if mode == "moba":
            jf = j.astype(F32)
            chosen = jnp.where(sel_sc[0] == jf, 0.0,
                               jnp.where(sel_sc[1] == jf, 0.0,
                                         jnp.where(sel_sc[2] == jf, 0.0, NEG_INF)))
            s = s + pltpu.repeat(chosen, reps, 1)
        m_prev = m_sc[...]
        m_new = jnp.maximum(m_prev, jnp.max(s, axis=1, keepdims=True))
        p = jnp.exp2(s - pltpu.repeat(m_new, reps, 1))
        alpha = jnp.exp2(m_prev - m_new)
        l_sc[...] = alpha * l_sc[...] + jnp.sum(p, axis=1, keepdims=True)
        acc_sc[...] = alpha * acc_sc[...] + jnp.dot(
            p.astype(BF16), v_ref[0, pl.ds(start, tk), :], preferred_element_type=F32)
        m_sc[...] = m_new
        return carry

    lax.fori_loop(0, jd, past_tile, 0)

    out = acc_sc[...] / l_sc[...]
    if mode == "diff":
        lam_rows = lam_ref[...]
        lam = (jnp.exp(jnp.sum(lam_rows[0:1] * lam_rows[1:2], axis=1, keepdims=True))
               - jnp.exp(jnp.sum(lam_rows[2:3] * lam_rows[3:4], axis=1, keepdims=True)) + lam_init)
        a = out[:tq] - lam * out[tq:]
        ms = jnp.mean(a * a, axis=-1, keepdims=True)
        out = a * lax.rsqrt(ms + EPS) * subg_ref[...] * (1.0 - lam_init)
    o_ref[0] = out.astype(o_ref.dtype)


def _attention(qkv, *, mode, n_heads, q_col, k_col, v_col, tq, tk, slopes=None, lam_rows=None,
               subln_g=None, cum=None, lam_init=0.0):
    bsz, seq, _ = qkv.shape
    rows = 2 * tq if mode == "diff" else tq
    q_spec = pl.BlockSpec((1, tq, HEAD_DIM), lambda b, h, i: (b, i, q_col + h))
    k_spec = pl.BlockSpec((1, seq, HEAD_DIM), lambda b, h, i: (b, 0, k_col + h))
    v_spec = pl.BlockSpec((1, seq, HEAD_DIM), lambda b, h, i: (b, 0, v_col + h))
    smem = pl.BlockSpec(memory_space=pltpu.SMEM)
    scratch = [pltpu.VMEM((rows, LANES), F32), pltpu.VMEM((rows, LANES), F32),
               pltpu.VMEM((rows, HEAD_DIM), F32)]
    if mode == "diff":
        in_specs = [smem, q_spec, k_spec, v_spec,
                    pl.BlockSpec((SUBLANES, LANES), lambda b, h, i: (0, 0)),
                    pl.BlockSpec((1, HEAD_DIM), lambda b, h, i: (0, 0))]
        args = [slopes, qkv, qkv, qkv, lam_rows, subln_g]
    elif mode == "moba":
        assert tq == MOBA_BLOCK and tk == MOBA_BLOCK and seq // MOBA_BLOCK <= LANES
        in_specs = [smem, q_spec, k_spec, v_spec]
        args = [slopes, qkv, qkv, qkv]
        scratch += [pltpu.VMEM((LANES, HEAD_DIM), F32), pltpu.VMEM((MOBA_TOPK, tq, LANES), F32)]
    else:
        in_specs = [q_spec, k_spec, v_spec,
                    pl.BlockSpec((1, 1, seq // tk, tk), lambda b, h, i: (b, h, 0, 0))]
        args = [qkv, qkv, qkv, cum]
    return pl.pallas_call(
        functools.partial(_attn_kernel, mode=mode, tq=tq, tk=tk, seq=seq, lam_init=lam_init),
        out_shape=jax.ShapeDtypeStruct((bsz, seq, n_heads * HEAD_DIM), BF16),
        grid=(bsz, n_heads, seq // tq),
        in_specs=in_specs,
        out_specs=pl.BlockSpec((1, tq, HEAD_DIM), lambda b, h, i: (b, i, h)),
        scratch_shapes=scratch,
        compiler_params=_compiler_params(("arbitrary", "arbitrary", "arbitrary")),
        name="attn_" + mode,
    )(*args)


def _route(logits_t, bias_t):
    aff = jax.nn.sigmoid(logits_t)
    sel = aff + bias_t
    sel_rows = [sel[e:e + 1, :] for e in range(N_EXPERTS)]
    aff_rows = [aff[e:e + 1, :] for e in range(N_EXPERTS)]

    def first_argmax(vals):
        idx = jnp.zeros_like(vals[0])
        top = vals[0]
        for r in range(1, len(vals)):
            better = vals[r] > top
            idx = jnp.where(better, float(r), idx)
            top = jnp.where(better, vals[r], top)
        return idx

    group_scores = []
    for g in range(N_GROUPS):
        a, b, c, d = sel_rows[EXPERTS_PER_GROUP * g:EXPERTS_PER_GROUP * (g + 1)]
        group_scores.append(jnp.maximum(jnp.maximum(jnp.maximum(a + b, a + c), jnp.maximum(a + d, b + c)),
                                        jnp.maximum(b + d, c + d)))
    best = first_argmax(group_scores)

    def in_best_group(rows_, r):
        out = rows_[(N_GROUPS - 1) * EXPERTS_PER_GROUP + r]
        for g in range(N_GROUPS - 2, -1, -1):
            out = jnp.where(best == float(g), rows_[g * EXPERTS_PER_GROUP + r], out)
        return out

    cand_sel = [in_best_group(sel_rows, r) for r in range(EXPERTS_PER_GROUP)]
    cand_aff = [in_best_group(aff_rows, r) for r in range(EXPERTS_PER_GROUP)]
    i1 = first_argmax(cand_sel)
    i2 = first_argmax([jnp.where(i1 == float(r), -jnp.inf, cand_sel[r])
                       for r in range(EXPERTS_PER_GROUP)])

    def pick(idx):
        out = cand_aff[EXPERTS_PER_GROUP - 1]
        for r in range(EXPERTS_PER_GROUP - 2, -1, -1):
            out = jnp.where(idx == float(r), cand_aff[r], out)
        return out

    w1, w2 = pick(i1), pick(i2)
    den = w1 + w2
    w1, w2 = w1 / den, w2 / den
    gate_rows = []
    for e in range(N_EXPERTS):
        g, r = divmod(e, EXPERTS_PER_GROUP)
        in_expert = jnp.where(i1 == float(r), w1, jnp.where(i2 == float(r), w2, 0.0))
        gate_rows.append(jnp.where(best == float(g), in_expert, 0.0))
    return jnp.concatenate(gate_rows, axis=0)


def _outproj_kernel(a_ref, b_ref, w_ref, x_ref, mod_ref, g_ref, rw_ref, rb_ref,
                    xo_ref, h_ref, gates_ref):
    half = a_ref.shape[1]
    y = (jnp.dot(a_ref[...], w_ref[0:half, :], preferred_element_type=F32)
         + jnp.dot(b_ref[...], w_ref[half:2 * half, :], preferred_element_type=F32))
    x_new = x_ref[...] + mod_ref[0, 2:3, :] * y
    xo_ref[...] = x_new
    h = _modulated_norm(x_new, g_ref[...], mod_ref, 3, 4)
    h_ref[...] = h.astype(BF16)
    logits = jnp.dot(h, rw_ref[...], precision=HIGHEST, preferred_element_type=F32)
    gates_t = _route(logits.T[0:N_EXPERTS, :], rb_ref[...])
    tm = gates_t.shape[1]
    padded = jnp.concatenate([gates_t, jnp.zeros((LANES - N_EXPERTS, tm), F32)], axis=0)
    gates_ref[...] = padded.T


def _outproj(parts, w_bf16, x2, mod, ffn_gain, router_w_pad, router_b_t, *, seq, tm=512):
    n_tok, d = x2.shape
    half = d // 2
    tiles_per_seq = seq // tm
    (a, ca), (b, cb) = parts
    return pl.pallas_call(
        _outproj_kernel,
        out_shape=[jax.ShapeDtypeStruct((n_tok, d), F32),
                   jax.ShapeDtypeStruct((n_tok, d), BF16),
                   jax.ShapeDtypeStruct((n_tok, LANES), F32)],
        grid=(n_tok // tm,),
        in_specs=[
            pl.BlockSpec((tm, half), lambda i: (i, ca)),
            pl.BlockSpec((tm, half), lambda i: (i, cb)),
            pl.BlockSpec((d, d), lambda i: (0, 0)),
            pl.BlockSpec((tm, d), lambda i: (i, 0)),
            pl.BlockSpec((1, 6, d), lambda i: (i // tiles_per_seq, 0, 0)),
            pl.BlockSpec((1, d), lambda i: (0, 0)),
            pl.BlockSpec((d, LANES), lambda i: (0, 0)),
            pl.BlockSpec((N_EXPERTS, tm), lambda i: (0, 0)),
        ],
        out_specs=[pl.BlockSpec((tm, d), lambda i: (i, 0)),
                   pl.BlockSpec((tm, d), lambda i: (i, 0)),
                   pl.BlockSpec((tm, LANES), lambda i: (i, 0))],
        compiler_params=_compiler_params(("arbitrary",)),
        name="outproj_router",
    )(a, b, w_bf16, x2, mod, ffn_gain.reshape(1, d), router_w_pad,
      jnp.broadcast_to(router_b_t, (N_EXPERTS, tm)))


def _moe_kernel(h_ref, gates_ref, w1_ref, w3_ref, w2_ref, x_ref, mod_ref, fg_ref, o_ref, acc_sc,
                *, final_norm):
    e = pl.program_id(1)

    @pl.when(e == 0)
    def _():
        acc_sc[...] = jnp.zeros_like(acc_sc)

    h = h_ref[...]
    u = jnp.dot(h, w1_ref[0], preferred_element_type=F32)
    v = jnp.dot(h, w3_ref[0], preferred_element_type=F32)
    gates = gates_ref[...]
    lane = lax.broadcasted_iota(jnp.int32, gates.shape, 1)
    gate_col = jnp.sum(jnp.where(lane == e, gates, 0.0), axis=1, keepdims=True)
    act = (u * jax.nn.sigmoid(u)) * v * gate_col
    acc_sc[...] += jnp.dot(act.astype(BF16), w2_ref[0], preferred_element_type=F32)

    @pl.when(e == pl.num_programs(1) - 1)
    def _():
        x_new = x_ref[...] + mod_ref[0, 5:6, :] * acc_sc[...]
        if final_norm:
            ms = jnp.mean(x_new * x_new, axis=-1, keepdims=True)
            x_new = x_new * lax.rsqrt(ms + EPS) * fg_ref[...]
        o_ref[...] = x_new


def _moe(h2, gates, w1, w3, w2, x2, mod, final_gain, *, seq, final_norm, tm=512):
    n_tok, d = x2.shape
    n_exp, _, d_ff = w1.shape
    tiles_per_seq = seq // tm
    return pl.pallas_call(
        functools.partial(_moe_kernel, final_norm=final_norm),
        out_shape=jax.ShapeDtypeStruct((n_tok, d), F32),
        grid=(n_tok // tm, n_exp),
        in_specs=[
            pl.BlockSpec((tm, d), lambda i, e: (i, 0)),
            pl.BlockSpec((tm, LANES), lambda i, e: (i, 0)),
            pl.BlockSpec((1, d, d_ff), lambda i, e: (e, 0, 0)),
            pl.BlockSpec((1, d, d_ff), lambda i, e: (e, 0, 0)),
            pl.BlockSpec((1, d_ff, d), lambda i, e: (e, 0, 0)),
            pl.BlockSpec((tm, d), lambda i, e: (i, 0)),
            pl.BlockSpec((1, 6, d), lambda i, e: (i // tiles_per_seq, 0, 0)),
            pl.BlockSpec((1, d), lambda i, e: (0, 0)),
        ],
        out_specs=pl.BlockSpec((tm, d), lambda i, e: (i, 0)),
        scratch_shapes=[pltpu.VMEM((tm, d), F32)],
        compiler_params=_compiler_params(("arbitrary", "arbitrary")),
        name="moe_final" if final_norm else "moe",
    )(h2, gates, w1, w3, w2, x2, mod, final_gain.reshape(1, d))


def _alibi_slopes(n):
    return 2.0 ** (-8.0 * np.arange(1, n + 1) / n)


def kernel(x, c, ada_w, ada_b, attn_norm_g, ffn_norm_g, w_in_even, w_out_even, lam_q1, lam_k1,
           lam_q2, lam_k2, subln_g, w_in_odd, b_forget, w_out_odd, router_w, router_b, moe_w1,
           moe_w3, moe_w2, final_norm_g):
    bsz, seq, d = x.shape
    depth = ada_w.shape[0]
    n_tok = bsz * seq
    x2 = x.reshape(n_tok, d)

    mod_all = _adaln(c, ada_w, ada_b)

    slopes = _alibi_slopes(A_HEADS + B_HEADS)
    slopes_a = jnp.asarray(slopes[0::2] * LOG2E, F32)
    slopes_b = jnp.asarray(slopes[1::2] * LOG2E, F32)
    a_width = A_HEADS * HEAD_DIM
    b_width = B_HEADS * HEAD_DIM
    c_width = C_HEADS * HEAD_DIM
    cs_even = np.ones((1, 3 * a_width + 3 * b_width), np.float32)
    cs_even[0, :a_width] = A_QK_DIM ** -0.5 * LOG2E
    cs_even[0, 3 * a_width:3 * a_width + b_width] = HEAD_DIM ** -0.5 * LOG2E
    cs_odd = np.ones((1, 3 * c_width), np.float32)
    cs_odd[0, :c_width] = HEAD_DIM ** -0.5 * LOG2E

    router_w_pad = jnp.pad(router_w, ((0, 0), (0, LANES - N_EXPERTS)))
    router_b_t = router_b.reshape(N_EXPERTS, 1)

    for l in range(depth):
        mod = mod_all[l]
        j = l // 2
        if l % 2 == 0:
            (qkv,) = _inproj(x2, attn_norm_g[l], mod, w_in_even[j].astype(BF16),
                             jnp.asarray(cs_even), seq=seq)
            qkv = qkv.reshape(bsz, seq, -1)
            lam_init = 0.8 - 0.6 * math.exp(-0.3 * l)
            lam_rows = jnp.zeros((SUBLANES, LANES), F32)
            lam_rows = lam_rows.at[0:4, 0:A_QK_DIM].set(
                jnp.stack([lam_q1[j], lam_k1[j], lam_q2[j], lam_k2[j]]).astype(F32))
            a_out = _attention(qkv, mode="diff", n_heads=A_HEADS, q_col=0, k_col=A_HEADS,
                               v_col=2 * A_HEADS, tq=256, tk=512, slopes=slopes_a,
                               lam_rows=lam_rows, subln_g=subln_g[j].reshape(1, HEAD_DIM),
                               lam_init=lam_init)
            b_out = _attention(qkv, mode="moba", n_heads=B_HEADS, q_col=3 * A_HEADS,
                               k_col=3 * A_HEADS + B_HEADS, v_col=3 * A_HEADS + 2 * B_HEADS,
                               tq=MOBA_BLOCK, tk=MOBA_BLOCK, slopes=slopes_b)
            parts = [(a_out.reshape(n_tok, a_width), 0), (b_out.reshape(n_tok, b_width), 0)]
            w_out = w_out_even[j]
        else:
            w_in = w_in_odd[j]
            wf = jnp.pad(w_in[:, 3 * c_width:], ((0, 0), (0, LANES - C_HEADS)))
            qkv, z = _inproj(x2, attn_norm_g[l], mod, w_in[:, :3 * c_width].astype(BF16),
                             jnp.asarray(cs_odd), wf, seq=seq)
            qkv = qkv.reshape(bsz, seq, -1)
            bf_row = jnp.pad(b_forget[j].astype(F32), (0, LANES - C_HEADS)).reshape(1, LANES)
            cum = _forget_cumsum(z.reshape(bsz, seq, LANES), bf_row)
            tk = 512
            cum = cum[:, :C_HEADS, :].reshape(bsz, C_HEADS, seq // tk, tk)
            c_out = _attention(qkv, mode="fox", n_heads=C_HEADS, q_col=0, k_col=C_HEADS,
                               v_col=2 * C_HEADS, tq=256, tk=tk, cum=cum)
            c_out = c_out.reshape(n_tok, c_width)
            parts = [(c_out, 0), (c_out, 1)]
            w_out = w_out_odd[j]
        x2, h2, gates = _outproj(parts, w_out.astype(BF16), x2, mod, ffn_norm_g[l], router_w_pad,
                                 router_b_t, seq=seq)
        x2 = _moe(h2, gates, moe_w1[l].astype(BF16), moe_w3[l].astype(BF16),
                  moe_w2[l].astype(BF16), x2, mod, final_norm_g, seq=seq,
                  final_norm=(l == depth - 1))
    return x2.reshape(bsz, seq, d)
```

```python
import functools
import math

import jax
import jax.numpy as jnp
import numpy as np
from jax import lax
from jax.experimental import pallas as pl
from jax.experimental.pallas import tpu as pltpu

F32 = jnp.float32
BF16 = jnp.bfloat16
HIGHEST = lax.Precision.HIGHEST

HEAD_DIM = 128
A_HEADS = 8
A_QK_DIM = 64
B_HEADS = 8
C_HEADS = 16
MOBA_BLOCK = 256
MOBA_TOPK = 3
N_EXPERTS = 16
N_GROUPS = 4
EXPERTS_PER_GROUP = N_EXPERTS // N_GROUPS
EPS = 1e-6
NEG_INF = -1e30
LOG2E = 1.4426950408889634
GROUP_LANE = N_EXPERTS

LANES = 128
SUBLANES = 8
VMEM_LIMIT_BYTES = 56 * 1024 * 1024

NT_DIMS = (((1,), (1,)), ((), ()))


def _compiler_params(semantics):
    return pltpu.CompilerParams(dimension_semantics=semantics, vmem_limit_bytes=VMEM_LIMIT_BYTES)


def _adaln_kernel(c_ref, w_ref, b_ref, o_ref):
    c = c_ref[...]
    cond = c * jax.nn.sigmoid(c)
    o_ref[0] = jnp.dot(cond, w_ref[0], precision=HIGHEST, preferred_element_type=F32) + b_ref[0]


def _adaln(c, ada_w, ada_b, *, tn=1024):
    depth, d, n = ada_w.shape
    bsz = c.shape[0]
    rows = -(-bsz // SUBLANES) * SUBLANES
    c_pad = jnp.pad(c, ((0, rows - bsz), (0, 0)))
    out = pl.pallas_call(
        _adaln_kernel,
        out_shape=jax.ShapeDtypeStruct((depth, rows, n), F32),
        grid=(depth, n // tn),
        in_specs=[
            pl.BlockSpec((rows, d), lambda l, j: (0, 0)),
            pl.BlockSpec((1, d, tn), lambda l, j: (l, 0, j)),
            pl.BlockSpec((1, 1, tn), lambda l, j: (l, 0, j)),
        ],
        out_specs=pl.BlockSpec((1, rows, tn), lambda l, j: (l, 0, j)),
        compiler_params=_compiler_params(("arbitrary", "arbitrary")),
        name="adaln",
    )(c_pad, ada_w, ada_b.reshape(depth, 1, n))
    return out[:, :bsz].reshape(depth, bsz, 6, d)


def _modulated_norm(x, gain_row, mod_ref, shift_idx, scale_idx):
    ms = jnp.mean(x * x, axis=-1, keepdims=True)
    xn = x * lax.rsqrt(ms + EPS)
    gain = gain_row * (1.0 + mod_ref[0, scale_idx:scale_idx + 1, :])
    return xn * gain + mod_ref[0, shift_idx:shift_idx + 1, :]


def _inproj_kernel(*refs, with_forget):
    if with_forget:
        x_ref, g_ref, mod_ref, w_ref, cs_ref, wf_ref, o_ref, z_ref, h_sc = refs
    else:
        x_ref, g_ref, mod_ref, w_ref, cs_ref, o_ref, h_sc = refs

    @pl.when(pl.program_id(1) == 0)
    def _():
        h = _modulated_norm(x_ref[...], g_ref[...], mod_ref, 0, 1)
        h_sc[...] = h.astype(BF16)
        if with_forget:
            z_ref[...] = jnp.dot(h, wf_ref[...], precision=HIGHEST, preferred_element_type=F32)

    acc = jnp.dot(h_sc[...], w_ref[...], preferred_element_type=F32)
    o_ref[...] = (acc * cs_ref[...]).astype(o_ref.dtype)


def _inproj(x2, gain, mod, w_bf16, col_scale, wf=None, *, seq, tm=1024, tn=1024):
    n_tok, d = x2.shape
    n_out = w_bf16.shape[1]
    tiles_per_seq = seq // tm
    with_forget = wf is not None
    in_specs = [
        pl.BlockSpec((tm, d), lambda i, j: (i, 0)),
        pl.BlockSpec((1, d), lambda i, j: (0, 0)),
        pl.BlockSpec((1, 6, d), lambda i, j: (i // tiles_per_seq, 0, 0)),
        pl.BlockSpec((d, tn), lambda i, j: (0, j)),
        pl.BlockSpec((1, tn), lambda i, j: (0, j)),
    ]
    args = [x2, gain.reshape(1, d), mod, w_bf16, col_scale]
    out_shape = [jax.ShapeDtypeStruct((n_tok, n_out), BF16)]
    out_specs = [pl.BlockSpec((tm, tn), lambda i, j: (i, j))]
    if with_forget:
        in_specs.append(pl.BlockSpec((d, LANES), lambda i, j: (0, 0)))
        args.append(wf)
        out_shape.append(jax.ShapeDtypeStruct((n_tok, LANES), F32))
        out_specs.append(pl.BlockSpec((tm, LANES), lambda i, j: (i, 0)))
    return pl.pallas_call(
        functools.partial(_inproj_kernel, with_forget=with_forget),
        out_shape=out_shape,
        grid=(n_tok // tm, n_out // tn),
        in_specs=in_specs,
        out_specs=out_specs,
        scratch_shapes=[pltpu.VMEM((tm, d), BF16)],
        compiler_params=_compiler_params(("arbitrary", "arbitrary")),
        name="inproj_forget" if with_forget else "inproj",
    )(*args)


def _fcum_kernel(z_ref, bf_ref, o_ref, carry_sc):
    @pl.when(pl.program_id(1) == 0)
    def _():
        carry_sc[...] = jnp.zeros_like(carry_sc)

    z = z_ref[0] + bf_ref[...]
    log_f = jnp.minimum(z, 0.0) - jnp.log1p(jnp.exp(-jnp.abs(z)))
    tc = z.shape[0]
    row = lax.broadcasted_iota(jnp.int32, (tc, tc), 0)
    col = lax.broadcasted_iota(jnp.int32, (tc, tc), 1)
    tri = jnp.where(row >= col, 1.0, 0.0).astype(F32)
    cum = jnp.dot(tri, log_f, precision=HIGHEST, preferred_element_type=F32) + carry_sc[...]
    carry_sc[...] = cum[tc - 1:tc, :]
    o_ref[0] = cum.T


def _forget_cumsum(z, b_forget_row, *, tc=256):
    bsz, seq, _ = z.shape
    return pl.pallas_call(
        _fcum_kernel,
        out_shape=jax.ShapeDtypeStruct((bsz, LANES, seq), F32),
        grid=(bsz, seq // tc),
        in_specs=[
            pl.BlockSpec((1, tc, LANES), lambda b, t: (b, t, 0)),
            pl.BlockSpec((1, LANES), lambda b, t: (0, 0)),
        ],
        out_specs=pl.BlockSpec((1, LANES, tc), lambda b, t: (b, 0, t)),
        scratch_shapes=[pltpu.VMEM((1, LANES), F32)],
        compiler_params=_compiler_params(("arbitrary", "arbitrary")),
        name="forget_cumsum",
    )(z, b_forget_row)


def _attn_kernel(*refs, mode, tq, tk, seq, lam_init):
    if mode == "diff":
        slopes_ref, q_ref, k_ref, v_ref, lam_ref, subg_ref, o_ref = refs[:7]
        m_sc, l_sc, acc_sc, sa_sc, sb_sc = refs[7:]
    elif mode == "moba":
        slopes_ref, q_ref, k_ref, v_ref, o_ref = refs[:5]
        m_sc, l_sc, acc_sc, sa_sc, sb_sc, kmean_sc, sel_sc = refs[5:]
    else:
        q_ref, k_ref, v_ref, cum_ref, o_ref = refs[:5]
        m_sc, l_sc, acc_sc, sa_sc, sb_sc = refs[5:]

    head = pl.program_id(1)
    qi = pl.program_id(2)
    q0 = qi * tq
    rows = 2 * tq if mode == "diff" else tq
    reps = tk // LANES
    n_diag = tq // tk

    q = q_ref[0]
    if mode == "diff":
        lane = lax.broadcasted_iota(jnp.int32, (tq, HEAD_DIM), 1)
        zero = jnp.zeros_like(q)
        q = jnp.concatenate([jnp.where(lane < A_QK_DIM, q, zero),
                             jnp.where(lane >= A_QK_DIM, q, zero)], axis=0)

    def col_bias(j):
        if mode == "fox":
            return cum_ref[0, 0, pl.ds(j, 1), :] * (-LOG2E)
        kpos = lax.broadcasted_iota(jnp.int32, (1, tk), 1) + (j * tk - q0)
        return kpos.astype(F32) * slopes_ref[head]

    def scores(j):
        start = pl.multiple_of(j * tk, tk)
        k = k_ref[0, pl.ds(start, tk), :]
        s = lax.dot_general(q, k, NT_DIMS, preferred_element_type=F32)
        return s + col_bias(j), start

    def chosen_bias(j):
        jf = j.astype(F32)
        return jnp.where(sel_sc[0] == jf, 0.0,
                         jnp.where(sel_sc[1] == jf, 0.0,
                                   jnp.where(sel_sc[2] == jf, 0.0, NEG_INF)))

    if mode == "moba":
        n_blocks = seq // MOBA_BLOCK
        sub_per_tile = tq // MOBA_BLOCK

        @pl.when(qi == 0)
        def _():
            kmean_sc[...] = jnp.zeros_like(kmean_sc)
            for n in range(n_blocks):
                blk = k_ref[0, n * MOBA_BLOCK:(n + 1) * MOBA_BLOCK, :].astype(F32)
                kmean_sc[n:n + 1, :] = jnp.mean(blk, axis=0, keepdims=True)

        gate = lax.dot_general(q.astype(F32), kmean_sc[...], NT_DIMS, precision=HIGHEST,
                               preferred_element_type=F32)
        blk_id = lax.broadcasted_iota(jnp.int32, (tq, LANES), 1).astype(F32)
        row_sub = lax.broadcasted_iota(jnp.int32, (tq, LANES), 0) // MOBA_BLOCK
        own = (row_sub + qi * sub_per_tile).astype(F32)
        gate = jnp.where(blk_id < own, gate, NEG_INF)
        for r in range(MOBA_TOPK):
            best = jnp.max(gate, axis=1, keepdims=True)
            pick = jnp.min(jnp.where(gate == best, blk_id, float(LANES)), axis=1, keepdims=True)
            sel_sc[r] = jnp.where(pick < own, pick, -1.0)
            gate = jnp.where(blk_id == pick, -jnp.inf, gate)

    def accumulate(s, start, first):
        v = v_ref[0, pl.ds(start, tk), :]
        if first:
            m_new = jnp.max(s, axis=1, keepdims=True)
            p = jnp.exp2(s - m_new)
            m_sc[...] = jnp.broadcast_to(m_new, (rows, LANES))
            l_sc[...] = jnp.broadcast_to(jnp.sum(p, axis=1, keepdims=True), (rows, LANES))
            acc_sc[...] = jnp.dot(p.astype(BF16), v, preferred_element_type=F32)
            return
        m_prev = m_sc[...]
        m_new = jnp.maximum(m_prev, jnp.max(s, axis=1, keepdims=True))
        p = jnp.exp2(s - pltpu.repeat(m_new, reps, 1))
        alpha = jnp.exp2(m_prev - m_new)
        l_sc[...] = alpha * l_sc[...] + jnp.sum(p, axis=1, keepdims=True)
        acc_sc[...] = alpha * acc_sc[...] + jnp.dot(p.astype(BF16), v, preferred_element_type=F32)
        m_sc[...] = m_new

    rpos = lax.broadcasted_iota(jnp.int32, (rows, tk), 0)
    if mode == "diff":
        rpos = jnp.where(rpos >= tq, rpos - tq, rpos)
    cpos = lax.broadcasted_iota(jnp.int32, (rows, tk), 1)
    jd = qi * n_diag
    for c in range(n_diag):
        s, start = scores(jd + c)
        if mode == "moba":
            row_blk = rpos // MOBA_BLOCK
            s_sel = s + pltpu.repeat(chosen_bias(jd + c), reps, 1)
            s = jnp.where(row_blk == c, jnp.where(cpos + c * tk <= rpos, s, NEG_INF),
                          jnp.where(row_blk > c, s_sel, NEG_INF))
        else:
            s = jnp.where(cpos + c * tk <= rpos, s, NEG_INF)
        accumulate(s, start, first=(c == 0))

    def consume(s, j):
        if mode == "moba":
            s = s + pltpu.repeat(chosen_bias(j), reps, 1)
        accumulate(s, pl.multiple_of(j * tk, tk), first=False)

    def past_pair(jj, carry):
        j0 = 2 * jj
        sb_sc[...] = scores(j0 + 1)[0]
        consume(sa_sc[...], j0)
        sa_sc[...] = scores(jnp.minimum(j0 + 2, jd - 2))[0]
        consume(sb_sc[...], j0 + 1)
        return carry

    sa_sc[...] = scores(0)[0]
    lax.fori_loop(0, qi * (n_diag // 2), past_pair, 0)

    out = acc_sc[...] / l_sc[...]
    if mode == "diff":
        lam_rows = lam_ref[...]
        lam = (jnp.exp(jnp.sum(lam_rows[0:1] * lam_rows[1:2], axis=1, keepdims=True))
               - jnp.exp(jnp.sum(lam_rows[2:3] * lam_rows[3:4], axis=1, keepdims=True)) + lam_init)
        a = out[:tq] - lam * out[tq:]
        ms = jnp.mean(a * a, axis=-1, keepdims=True)
        out = a * lax.rsqrt(ms + EPS) * subg_ref[...] * (1.0 - lam_init)
    o_ref[0] = out.astype(o_ref.dtype)


ATTN_TILES = {"diff": (1024, 512), "moba": (1024, MOBA_BLOCK), "fox": (1024, 512)}


def _attention(qkv, *, mode, n_heads, q_col, k_col, v_col, slopes=None, lam_rows=None,
               subln_g=None, cum=None, lam_init=0.0):
    bsz, seq, _ = qkv.shape
    tq, tk = ATTN_TILES[mode]
    tq = min(tq, seq)
    assert tq % (2 * tk) == 0 and seq % tq == 0
    rows = 2 * tq if mode == "diff" else tq
    q_spec = pl.BlockSpec((1, tq, HEAD_DIM), lambda b, h, i: (b, i, q_col + h))
    k_spec = pl.BlockSpec((1, seq, HEAD_DIM), lambda b, h, i: (b, 0, k_col + h))
    v_spec = pl.BlockSpec((1, seq, HEAD_DIM), lambda b, h, i: (b, 0, v_col + h))
    smem = pl.BlockSpec(memory_space=pltpu.SMEM)
    scratch = [pltpu.VMEM((rows, LANES), F32), pltpu.VMEM((rows, LANES), F32),
               pltpu.VMEM((rows, HEAD_DIM), F32),
               pltpu.VMEM((rows, tk), F32), pltpu.VMEM((rows, tk), F32)]
    if mode == "diff":
        in_specs = [smem, q_spec, k_spec, v_spec,
                    pl.BlockSpec((SUBLANES, LANES), lambda b, h, i: (0, 0)),
                    pl.BlockSpec((1, HEAD_DIM), lambda b, h, i: (0, 0))]
        args = [slopes, qkv, qkv, qkv, lam_rows, subln_g]
    elif mode == "moba":
        assert tk == MOBA_BLOCK and seq // MOBA_BLOCK <= LANES
        in_specs = [smem, q_spec, k_spec, v_spec]
        args = [slopes, qkv, qkv, qkv]
        scratch += [pltpu.VMEM((LANES, HEAD_DIM), F32), pltpu.VMEM((MOBA_TOPK, tq, LANES), F32)]
    else:
        in_specs = [q_spec, k_spec, v_spec,
                    pl.BlockSpec((1, 1, seq // tk, tk), lambda b, h, i: (b, h, 0, 0))]
        args = [qkv, qkv, qkv, cum]
    return pl.pallas_call(
        functools.partial(_attn_kernel, mode=mode, tq=tq, tk=tk, seq=seq, lam_init=lam_init),
        out_shape=jax.ShapeDtypeStruct((bsz, seq, n_heads * HEAD_DIM), BF16),
        grid=(bsz, n_heads, seq // tq),
        in_specs=in_specs,
        out_specs=pl.BlockSpec((1, tq, HEAD_DIM), lambda b, h, i: (b, i, h)),
        scratch_shapes=scratch,
        compiler_params=_compiler_params(("arbitrary", "arbitrary", "arbitrary")),
        name="attn_" + mode,
    )(*args)


def _route(logits_t, bias_t):
    aff = jax.nn.sigmoid(logits_t)
    sel = aff + bias_t
    sel_rows = [sel[e:e + 1, :] for e in range(N_EXPERTS)]
    aff_rows = [aff[e:e + 1, :] for e in range(N_EXPERTS)]

    def first_argmax(vals):
        idx = jnp.zeros_like(vals[0])
        top = vals[0]
        for r in range(1, len(vals)):
            better = vals[r] > top
            idx = jnp.where(better, float(r), idx)
            top = jnp.where(better, vals[r], top)
        return idx

    group_scores = []
    for g in range(N_GROUPS):
        a, b, c, d = sel_rows[EXPERTS_PER_GROUP * g:EXPERTS_PER_GROUP * (g + 1)]
        group_scores.append(jnp.maximum(jnp.maximum(jnp.maximum(a + b, a + c), jnp.maximum(a + d, b + c)),
                                        jnp.maximum(b + d, c + d)))
    best = first_argmax(group_scores)

    def in_best_group(rows_, r):
        out = rows_[(N_GROUPS - 1) * EXPERTS_PER_GROUP + r]
        for g in range(N_GROUPS - 2, -1, -1):
            out = jnp.where(best == float(g), rows_[g * EXPERTS_PER_GROUP + r], out)
        return out

    cand_sel = [in_best_group(sel_rows, r) for r in range(EXPERTS_PER_GROUP)]
    cand_aff = [in_best_group(aff_rows, r) for r in range(EXPERTS_PER_GROUP)]
    i1 = first_argmax(cand_sel)
    i2 = first_argmax([jnp.where(i1 == float(r), -jnp.inf, cand_sel[r])
                       for r in range(EXPERTS_PER_GROUP)])

    def pick(idx):
        out = cand_aff[EXPERTS_PER_GROUP - 1]
        for r in range(EXPERTS_PER_GROUP - 2, -1, -1):
            out = jnp.where(idx == float(r), cand_aff[r], out)
        return out

    w1, w2 = pick(i1), pick(i2)
    den = w1 + w2
    w1, w2 = w1 / den, w2 / den
    gate_rows = []
    for e in range(N_EXPERTS):
        g, r = divmod(e, EXPERTS_PER_GROUP)
        in_expert = jnp.where(i1 == float(r), w1, jnp.where(i2 == float(r), w2, 0.0))
        gate_rows.append(jnp.where(best == float(g), in_expert, 0.0))
    return jnp.concatenate(gate_rows, axis=0), best


def _outproj_kernel(a_ref, b_ref, w_ref, x_ref, mod_ref, g_ref, rw_ref, rb_ref,
                    xo_ref, h_ref, gates_ref, group_ref):
    half = a_ref.shape[1]
    y = (jnp.dot(a_ref[...], w_ref[0:half, :], preferred_element_type=F32)
         + jnp.dot(b_ref[...], w_ref[half:2 * half, :], preferred_element_type=F32))
    x_new = x_ref[...] + mod_ref[0, 2:3, :] * y
    xo_ref[...] = x_new
    h = _modulated_norm(x_new, g_ref[...], mod_ref, 3, 4)
    h_ref[...] = h.astype(BF16)
    logits = jnp.dot(h, rw_ref[...], precision=HIGHEST, preferred_element_type=F32)
    gates_t, best = _route(logits.T[0:N_EXPERTS, :], rb_ref[...])
    tm = gates_t.shape[1]
    padded = jnp.concatenate(
        [gates_t, best, jnp.zeros((LANES - N_EXPERTS - 1, tm), F32)], axis=0)
    gates_ref[...] = padded.T
    group_ref[0] = best


def _outproj(parts, w_bf16, x2, mod, ffn_gain, router_w_pad, router_b_t, *, seq, tm=512):
    n_tok, d = x2.shape
    half = d // 2
    tiles_per_seq = seq // tm
    (a, ca), (b, cb) = parts
    return pl.pallas_call(
        _outproj_kernel,
        out_shape=[jax.ShapeDtypeStruct((n_tok, d), F32),
                   jax.ShapeDtypeStruct((n_tok, d), BF16),
                   jax.ShapeDtypeStruct((n_tok, LANES), F32),
                   jax.ShapeDtypeStruct((n_tok // tm, 1, tm), F32)],
        grid=(n_tok // tm,),
        in_specs=[
            pl.BlockSpec((tm, half), lambda i: (i, ca)),
            pl.BlockSpec((tm, half), lambda i: (i, cb)),
            pl.BlockSpec((d, d), lambda i: (0, 0)),
            pl.BlockSpec((tm, d), lambda i: (i, 0)),
            pl.BlockSpec((1, 6, d), lambda i: (i // tiles_per_seq, 0, 0)),
            pl.BlockSpec((1, d), lambda i: (0, 0)),
            pl.BlockSpec((d, LANES), lambda i: (0, 0)),
            pl.BlockSpec((N_EXPERTS, tm), lambda i: (0, 0)),
        ],
        out_specs=[pl.BlockSpec((tm, d), lambda i: (i, 0)),
                   pl.BlockSpec((tm, d), lambda i: (i, 0)),
                   pl.BlockSpec((tm, LANES), lambda i: (i, 0)),
                   pl.BlockSpec((1, 1, tm), lambda i: (i, 0, 0))],
        compiler_params=_compiler_params(("arbitrary",)),
        name="outproj_router",
    )(a, b, w_bf16, x2, mod, ffn_gain.reshape(1, d), router_w_pad,
      jnp.broadcast_to(router_b_t, (N_EXPERTS, tm)))


MOE_CHUNK = 256


def _split_bf16x3(x):
    hi = x.astype(BF16)
    r1 = x - hi.astype(F32)
    mid = r1.astype(BF16)
    lo = (r1 - mid.astype(F32)).astype(BF16)
    return hi, mid, lo


def _moe_kernel(h_ref, gates_ref, group_ref, w1_ref, w3_ref, w2_ref, o_ref, acc_sc):
    g = pl.program_id(0)
    gf = g.astype(F32)
    tm = h_ref.shape[0]
    n_local = w1_ref.shape[0]

    gates = gates_ref[...]
    lane = lax.broadcasted_iota(jnp.int32, (tm, LANES), 1)
    grp_col = jnp.sum(jnp.where(lane == GROUP_LANE, gates, 0.0), axis=1, keepdims=True)
    member_col = jnp.broadcast_to(jnp.where(grp_col == gf, 1.0, 0.0), (tm, LANES))
    member_row = jnp.broadcast_to(jnp.where(group_ref[0] == gf, 1.0, 0.0), (SUBLANES, tm))

    ri = lax.broadcasted_iota(jnp.int32, (tm, tm), 0)
    ci = lax.broadcasted_iota(jnp.int32, (tm, tm), 1)
    lower = jnp.where(ci < ri, 1.0, 0.0).astype(BF16)
    upper = jnp.where(ri < ci, 1.0, 0.0).astype(BF16)
    rank_col = jnp.dot(lower, member_col.astype(BF16), preferred_element_type=F32)
    rank_row = jnp.dot(member_row.astype(BF16), upper, preferred_element_type=F32)[0:1, :]
    pos_col = jnp.where(member_col > 0.0, rank_col, -1.0)
    pos_row = jnp.where(member_row[0:1, :] > 0.0, rank_row, -1.0)
    count = jnp.sum(member_row[0:1, :]).astype(jnp.int32)
    n_chunks = lax.shift_right_logical(count + (MOE_CHUNK - 1), MOE_CHUNK.bit_length() - 1)

    gates3 = jnp.concatenate(_split_bf16x3(gates), axis=1)
    acc_sc[...] = jnp.zeros_like(acc_sc)

    def chunk(k, carry):
        base = (k * MOE_CHUNK).astype(F32)
        slot_r = lax.broadcasted_iota(jnp.int32, (MOE_CHUNK, tm), 0).astype(F32) + base
        slot_c = lax.broadcasted_iota(jnp.int32, (tm, MOE_CHUNK), 1).astype(F32) + base
        gather = jnp.where(pos_row == slot_r, 1.0, 0.0).astype(BF16)
        scatter = jnp.where(pltpu.repeat(pos_col, MOE_CHUNK // LANES, 1) == slot_c,
                            1.0, 0.0).astype(BF16)
        hc = jnp.dot(gather, h_ref[...], preferred_element_type=F32).astype(BF16)
        g3 = jnp.dot(gather, gates3, preferred_element_type=F32)
        gc = g3[:, 0:LANES] + g3[:, LANES:2 * LANES] + g3[:, 2 * LANES:3 * LANES]
        clane = lax.broadcasted_iota(jnp.int32, (MOE_CHUNK, LANES), 1)
        y = None
        for el in range(n_local):
            gate_col = jnp.sum(jnp.where(clane == g * n_local + el, gc, 0.0), axis=1, keepdims=True)
            u = jnp.dot(hc, w1_ref[el], preferred_element_type=F32)
            v = jnp.dot(hc, w3_ref[el], preferred_element_type=F32)
            act = ((u * jax.nn.sigmoid(u)) * v * gate_col).astype(BF16)
            part = jnp.dot(act, w2_ref[el], preferred_element_type=F32)
            y = part if y is None else y + part
        acc_sc[...] += jnp.dot(scatter, y.astype(BF16), preferred_element_type=F32)
        return carry

    lax.fori_loop(0, n_chunks, chunk, 0)
    o_ref[0] = acc_sc[...].astype(o_ref.dtype)


def _moe(h2, gates, group_rows, w1, w3, w2, *, tm=1024):
    n_tok, d = h2.shape
    n_exp, _, d_ff = w1.shape
    tm = min(tm, n_tok)
    per_group = n_exp // N_GROUPS
    once = pl.Buffered(1)
    return pl.pallas_call(
        _moe_kernel,
        out_shape=jax.ShapeDtypeStruct((N_GROUPS, n_tok, d), BF16),
        grid=(N_GROUPS, n_tok // tm),
        in_specs=[
            pl.BlockSpec((tm, d), lambda g, i: (i, 0), pipeline_mode=once),
            pl.BlockSpec((tm, LANES), lambda g, i: (i, 0)),
            pl.BlockSpec((1, 1, tm), lambda g, i: (i, 0, 0)),
            pl.BlockSpec((per_group, d, d_ff), lambda g, i: (g, 0, 0), pipeline_mode=once),
            pl.BlockSpec((per_group, d, d_ff), lambda g, i: (g, 0, 0), pipeline_mode=once),
            pl.BlockSpec((per_group, d_ff, d), lambda g, i: (g, 0, 0), pipeline_mode=once),
        ],
        out_specs=pl.BlockSpec((1, tm, d), lambda g, i: (g, i, 0)),
        scratch_shapes=[pltpu.VMEM((tm, d), F32)],
        compiler_params=_compiler_params(("arbitrary", "arbitrary")),
        name="moe_grouped",
    )(h2, gates, group_rows.reshape(n_tok // tm, 1, tm), w1, w3, w2)


def _combine_kernel(x_ref, y_ref, mod_ref, fg_ref, o_ref, *, final_norm):
    y = y_ref[0].astype(F32)
    for g in range(1, y_ref.shape[0]):
        y = y + y_ref[g].astype(F32)
    x_new = x_ref[...] + mod_ref[0, 5:6, :] * y
    if final_norm:
        ms = jnp.mean(x_new * x_new, axis=-1, keepdims=True)
        x_new = x_new * lax.rsqrt(ms + EPS) * fg_ref[...]
    o_ref[...] = x_new


def _combine(x2, y, mod, final_gain, *, seq, final_norm, tm=512):
    n_tok, d = x2.shape
    tm = min(tm, seq)
    tiles_per_seq = seq // tm
    return pl.pallas_call(
        functools.partial(_combine_kernel, final_norm=final_norm),
        out_shape=jax.ShapeDtypeStruct((n_tok, d), F32),
        grid=(n_tok // tm,),
        in_specs=[
            pl.BlockSpec((tm, d), lambda i: (i, 0)),
            pl.BlockSpec((y.shape[0], tm, d), lambda i: (0, i, 0)),
            pl.BlockSpec((1, 6, d), lambda i: (i // tiles_per_seq, 0, 0)),
            pl.BlockSpec((1, d), lambda i: (0, 0)),
        ],
        out_specs=pl.BlockSpec((tm, d), lambda i: (i, 0)),
        compiler_params=_compiler_params(("arbitrary",)),
        name="combine_final" if final_norm else "combine",
    )(x2, y, mod, final_gain.reshape(1, d))


def _alibi_slopes(n):
    return 2.0 ** (-8.0 * np.arange(1, n + 1) / n)


def kernel(x, c, ada_w, ada_b, attn_norm_g, ffn_norm_g, w_in_even, w_out_even, lam_q1, lam_k1,
           lam_q2, lam_k2, subln_g, w_in_odd, b_forget, w_out_odd, router_w, router_b, moe_w1,
           moe_w3, moe_w2, final_norm_g):
    bsz, seq, d = x.shape
    depth = ada_w.shape[0]
    n_tok = bsz * seq
    x2 = x.reshape(n_tok, d)

    mod_all = _adaln(c, ada_w, ada_b)

    slopes = _alibi_slopes(A_HEADS + B_HEADS)
    slopes_a = jnp.asarray(slopes[0::2] * LOG2E, F32)
    slopes_b = jnp.asarray(slopes[1::2] * LOG2E, F32)
    a_width = A_HEADS * HEAD_DIM
    b_width = B_HEADS * HEAD_DIM
    c_width = C_HEADS * HEAD_DIM
    cs_even = np.ones((1, 3 * a_width + 3 * b_width), np.float32)
    cs_even[0, :a_width] = A_QK_DIM ** -0.5 * LOG2E
    cs_even[0, 3 * a_width:3 * a_width + b_width] = HEAD_DIM ** -0.5 * LOG2E
    cs_odd = np.ones((1, 3 * c_width), np.float32)
    cs_odd[0, :c_width] = HEAD_DIM ** -0.5 * LOG2E

    router_w_pad = jnp.pad(router_w, ((0, 0), (0, LANES - N_EXPERTS)))
    router_b_t = router_b.reshape(N_EXPERTS, 1)

    for l in range(depth):
        mod = mod_all[l]
        j = l // 2
        if l % 2 == 0:
            (qkv,) = _inproj(x2, attn_norm_g[l], mod, w_in_even[j].astype(BF16),
                             jnp.asarray(cs_even), seq=seq)
            qkv = qkv.reshape(bsz, seq, -1)
            lam_init = 0.8 - 0.6 * math.exp(-0.3 * l)
            lam_rows = jnp.zeros((SUBLANES, LANES), F32)
            lam_rows = lam_rows.at[0:4, 0:A_QK_DIM].set(
                jnp.stack([lam_q1[j], lam_k1[j], lam_q2[j], lam_k2[j]]).astype(F32))
            a_out = _attention(qkv, mode="diff", n_heads=A_HEADS, q_col=0, k_col=A_HEADS,
                               v_col=2 * A_HEADS, slopes=slopes_a,
                               lam_rows=lam_rows, subln_g=subln_g[j].reshape(1, HEAD_DIM),
                               lam_init=lam_init)
            b_out = _attention(qkv, mode="moba", n_heads=B_HEADS, q_col=3 * A_HEADS,
                               k_col=3 * A_HEADS + B_HEADS, v_col=3 * A_HEADS + 2 * B_HEADS,
                               slopes=slopes_b)
            parts = [(a_out.reshape(n_tok, a_width), 0), (b_out.reshape(n_tok, b_width), 0)]
            w_out = w_out_even[j]
        else:
            w_in = w_in_odd[j]
            wf = jnp.pad(w_in[:, 3 * c_width:], ((0, 0), (0, LANES - C_HEADS)))
            qkv, z = _inproj(x2, attn_norm_g[l], mod, w_in[:, :3 * c_width].astype(BF16),
                             jnp.asarray(cs_odd), wf, seq=seq)
            qkv = qkv.reshape(bsz, seq, -1)
            bf_row = jnp.pad(b_forget[j].astype(F32), (0, LANES - C_HEADS)).reshape(1, LANES)
            cum = _forget_cumsum(z.reshape(bsz, seq, LANES), bf_row)
            tk = ATTN_TILES["fox"][1]
            cum = cum[:, :C_HEADS, :].reshape(bsz, C_HEADS, seq // tk, tk)
            c_out = _attention(qkv, mode="fox", n_heads=C_HEADS, q_col=0, k_col=C_HEADS,
                               v_col=2 * C_HEADS, cum=cum)
            c_out = c_out.reshape(n_tok, c_width)
            parts = [(c_out, 0), (c_out, 1)]
            w_out = w_out_odd[j]
        x2, h2, gates, group_rows = _outproj(parts, w_out.astype(BF16), x2, mod, ffn_norm_g[l],
                                             router_w_pad, router_b_t, seq=seq)
        y = _moe(h2, gates, group_rows, moe_w1[l].astype(BF16), moe_w3[l].astype(BF16),
                 moe_w2[l].astype(BF16))
        x2 = _combine(x2, y, mod, final_norm_g, seq=seq, final_norm=(l == depth - 1))
    return x2.reshape(bsz, seq, d)
```

```python
import functools
import math

import jax
import jax.numpy as jnp
import numpy as np
from jax import lax
from jax.experimental import pallas as pl
from jax.experimental.pallas import tpu as pltpu

F32 = jnp.float32
BF16 = jnp.bfloat16
HIGHEST = lax.Precision.HIGHEST

HEAD_DIM = 128
A_HEADS = 8
A_QK_DIM = 64
B_HEADS = 8
C_HEADS = 16
MOBA_BLOCK = 256
MOBA_TOPK = 3
N_EXPERTS = 16
N_GROUPS = 4
EXPERTS_PER_GROUP = N_EXPERTS // N_GROUPS
EPS = 1e-6
NEG_INF = -1e30
LOG2E = 1.4426950408889634
GROUP_LANE = N_EXPERTS

LANES = 128
SUBLANES = 8
VMEM_LIMIT_BYTES = 56 * 1024 * 1024

NT_DIMS = (((1,), (1,)), ((), ()))


def _lane_repeat(x, n):
    return x if n == 1 else jnp.concatenate([x] * n, axis=1)


def _compiler_params(semantics):
    return pltpu.CompilerParams(dimension_semantics=semantics, vmem_limit_bytes=VMEM_LIMIT_BYTES)


def _adaln_kernel(c_ref, w_ref, b_ref, o_ref):
    c = c_ref[...]
    cond = c * jax.nn.sigmoid(c)
    o_ref[0] = jnp.dot(cond, w_ref[0], precision=HIGHEST, preferred_element_type=F32) + b_ref[0]


def _adaln(c, ada_w, ada_b, *, tn=1024):
    depth, d, n = ada_w.shape
    bsz = c.shape[0]
    rows = -(-bsz // SUBLANES) * SUBLANES
    c_pad = jnp.pad(c, ((0, rows - bsz), (0, 0)))
    out = pl.pallas_call(
        _adaln_kernel,
        out_shape=jax.ShapeDtypeStruct((depth, rows, n), F32),
        grid=(depth, n // tn),
        in_specs=[
            pl.BlockSpec((rows, d), lambda l, j: (0, 0)),
            pl.BlockSpec((1, d, tn), lambda l, j: (l, 0, j)),
            pl.BlockSpec((1, 1, tn), lambda l, j: (l, 0, j)),
        ],
        out_specs=pl.BlockSpec((1, rows, tn), lambda l, j: (l, 0, j)),
        compiler_params=_compiler_params(("arbitrary", "arbitrary")),
        name="adaln",
    )(c_pad, ada_w, ada_b.reshape(depth, 1, n))
    return out[:, :bsz].reshape(depth, bsz, 6, d)


def _modulated_norm(x, gain_row, mod_ref, shift_idx, scale_idx):
    ms = jnp.mean(x * x, axis=-1, keepdims=True)
    xn = x * lax.rsqrt(ms + EPS)
    gain = gain_row * (1.0 + mod_ref[0, scale_idx:scale_idx + 1, :])
    return xn * gain + mod_ref[0, shift_idx:shift_idx + 1, :]


def _inproj_kernel(*refs, with_forget):
    if with_forget:
        x_ref, g_ref, mod_ref, w_ref, cs_ref, wf_ref, o_ref, z_ref, h_sc = refs
    else:
        x_ref, g_ref, mod_ref, w_ref, cs_ref, o_ref, h_sc = refs

    @pl.when(pl.program_id(1) == 0)
    def _():
        h = _modulated_norm(x_ref[...], g_ref[...], mod_ref, 0, 1)
        h_sc[...] = h.astype(BF16)
        if with_forget:
            z_ref[...] = jnp.dot(h, wf_ref[...], precision=HIGHEST, preferred_element_type=F32)

    acc = jnp.dot(h_sc[...], w_ref[...], preferred_element_type=F32)
    o_ref[...] = (acc * cs_ref[...]).astype(o_ref.dtype)


def _inproj(x2, gain, mod, w_bf16, col_scale, wf=None, *, seq, tm=1024, tn=1024):
    n_tok, d = x2.shape
    n_out = w_bf16.shape[1]
    tiles_per_seq = seq // tm
    with_forget = wf is not None
    in_specs = [
        pl.BlockSpec((tm, d), lambda i, j: (i, 0)),
        pl.BlockSpec((1, d), lambda i, j: (0, 0)),
        pl.BlockSpec((1, 6, d), lambda i, j: (i // tiles_per_seq, 0, 0)),
        pl.BlockSpec((d, tn), lambda i, j: (0, j)),
        pl.BlockSpec((1, tn), lambda i, j: (0, j)),
    ]
    args = [x2, gain.reshape(1, d), mod, w_bf16, col_scale]
    out_shape = [jax.ShapeDtypeStruct((n_tok, n_out), BF16)]
    out_specs = [pl.BlockSpec((tm, tn), lambda i, j: (i, j))]
    if with_forget:
        in_specs.append(pl.BlockSpec((d, LANES), lambda i, j: (0, 0)))
        args.append(wf)
        out_shape.append(jax.ShapeDtypeStruct((n_tok, LANES), F32))
        out_specs.append(pl.BlockSpec((tm, LANES), lambda i, j: (i, 0)))
    return pl.pallas_call(
        functools.partial(_inproj_kernel, with_forget=with_forget),
        out_shape=out_shape,
        grid=(n_tok // tm, n_out // tn),
        in_specs=in_specs,
        out_specs=out_specs,
        scratch_shapes=[pltpu.VMEM((tm, d), BF16)],
        compiler_params=_compiler_params(("arbitrary", "arbitrary")),
        name="inproj_forget" if with_forget else "inproj",
    )(*args)


def _fcum_kernel(z_ref, bf_ref, o_ref, carry_sc):
    @pl.when(pl.program_id(1) == 0)
    def _():
        carry_sc[...] = jnp.zeros_like(carry_sc)

    z = z_ref[0] + bf_ref[...]
    log_f = jnp.minimum(z, 0.0) - jnp.log1p(jnp.exp(-jnp.abs(z)))
    tc = z.shape[0]
    row = lax.broadcasted_iota(jnp.int32, (tc, tc), 0)
    col = lax.broadcasted_iota(jnp.int32, (tc, tc), 1)
    tri = jnp.where(row >= col, 1.0, 0.0).astype(F32)
    cum = jnp.dot(tri, log_f, precision=HIGHEST, preferred_element_type=F32) + carry_sc[...]
    carry_sc[...] = cum[tc - 1:tc, :]
    o_ref[0] = cum.T


def _forget_cumsum(z, b_forget_row, *, tc=256):
    bsz, seq, _ = z.shape
    return pl.pallas_call(
        _fcum_kernel,
        out_shape=jax.ShapeDtypeStruct((bsz, LANES, seq), F32),
        grid=(bsz, seq // tc),
        in_specs=[
            pl.BlockSpec((1, tc, LANES), lambda b, t: (b, t, 0)),
            pl.BlockSpec((1, LANES), lambda b, t: (0, 0)),
        ],
        out_specs=pl.BlockSpec((1, LANES, tc), lambda b, t: (b, 0, t)),
        scratch_shapes=[pltpu.VMEM((1, LANES), F32)],
        compiler_params=_compiler_params(("arbitrary", "arbitrary")),
        name="forget_cumsum",
    )(z, b_forget_row)


def _attn_kernel(*refs, mode, tq, tk, seq, lam_init):
    if mode == "diff":
        slopes_ref, q_ref, k_ref, v_ref, lam_ref, subg_ref, o_ref = refs[:7]
        m_sc, l_sc, acc_sc, sa_sc, sb_sc, vx_sc = refs[7:]
    elif mode == "moba":
        slopes_ref, q_ref, k_ref, v_ref, o_ref = refs[:5]
        m_sc, l_sc, acc_sc, sa_sc, sb_sc, vx_sc, kmean_sc, sel_sc = refs[5:]
    else:
        q_ref, k_ref, v_ref, cum_ref, o_ref = refs[:5]
        m_sc, l_sc, acc_sc, sa_sc, sb_sc, vx_sc = refs[5:]

    head = pl.program_id(1)
    qi = pl.program_id(2)
    q0 = qi * tq
    rows = 2 * tq if mode == "diff" else tq
    reps = tk // LANES
    n_diag = tq // tk

    q = q_ref[0]
    if mode == "diff":
        lane = lax.broadcasted_iota(jnp.int32, (tq, HEAD_DIM), 1)
        zero = jnp.zeros_like(q)
        q = jnp.concatenate([jnp.where(lane < A_QK_DIM, q, zero),
                             jnp.where(lane >= A_QK_DIM, q, zero)], axis=0)

    def col_bias(j):
        if mode == "fox":
            return cum_ref[0, 0, pl.ds(j, 1), :] * (-LOG2E)
        kpos = lax.broadcasted_iota(jnp.int32, (1, tk), 1) + (j * tk - q0)
        return kpos.astype(F32) * slopes_ref[head]

    def scores(j):
        start = pl.multiple_of(j * tk, tk)
        k = k_ref[0, pl.ds(start, tk), :]
        s = lax.dot_general(q, k, NT_DIMS, preferred_element_type=F32)
        return s + col_bias(j), start

    def chosen_bias(j):
        jf = j.astype(F32)
        return jnp.where(sel_sc[0] == jf, 0.0,
                         jnp.where(sel_sc[1] == jf, 0.0,
                                   jnp.where(sel_sc[2] == jf, 0.0, NEG_INF)))

    if mode == "moba":
        n_blocks = seq // MOBA_BLOCK
        sub_per_tile = tq // MOBA_BLOCK

        @pl.when(qi == 0)
        def _():
            kmean_sc[...] = jnp.zeros_like(kmean_sc)
            for n in range(n_blocks):
                blk = k_ref[0, n * MOBA_BLOCK:(n + 1) * MOBA_BLOCK, :].astype(F32)
                kmean_sc[n:n + 1, :] = jnp.mean(blk, axis=0, keepdims=True)

        gate = lax.dot_general(q.astype(F32), kmean_sc[...], NT_DIMS, precision=HIGHEST,
                               preferred_element_type=F32)
        blk_id = lax.broadcasted_iota(jnp.int32, (tq, LANES), 1).astype(F32)
        row_sub = lax.broadcasted_iota(jnp.int32, (tq, LANES), 0) // MOBA_BLOCK
        own = (row_sub + qi * sub_per_tile).astype(F32)
        gate = jnp.where(blk_id < own, gate, NEG_INF)
        for r in range(MOBA_TOPK):
            best = jnp.max(gate, axis=1, keepdims=True)
            pick = jnp.min(jnp.where(gate == best, blk_id, float(LANES)), axis=1, keepdims=True)
            sel_sc[r] = jnp.where(pick < own, pick, -1.0)
            gate = jnp.where(blk_id == pick, -jnp.inf, gate)

    @pl.when(qi == 0)
    def _():
        vx_sc[:, 0:HEAD_DIM] = v_ref[0]
        vx_sc[:, HEAD_DIM:2 * HEAD_DIM] = jnp.ones((seq, LANES), BF16)

    def accumulate(s, start, first):
        v = vx_sc[pl.ds(start, tk), :]
        if first:
            m_new = jnp.max(s, axis=1, keepdims=True)
            p = jnp.exp2(s - m_new)
            pv = jnp.dot(p.astype(BF16), v, preferred_element_type=F32)
            m_sc[...] = jnp.broadcast_to(m_new, (rows, LANES))
            l_sc[...] = pv[:, HEAD_DIM:2 * HEAD_DIM]
            acc_sc[...] = pv[:, 0:HEAD_DIM]
            return
        m_prev = m_sc[...]
        m_new = jnp.maximum(m_prev, jnp.max(s, axis=1, keepdims=True))
        p = jnp.exp2(s - _lane_repeat(m_new, reps))
        alpha = jnp.exp2(m_prev - m_new)
        pv = jnp.dot(p.astype(BF16), v, preferred_element_type=F32)
        l_sc[...] = alpha * l_sc[...] + pv[:, HEAD_DIM:2 * HEAD_DIM]
        acc_sc[...] = alpha * acc_sc[...] + pv[:, 0:HEAD_DIM]
        m_sc[...] = m_new

    rpos = lax.broadcasted_iota(jnp.int32, (rows, tk), 0)
    if mode == "diff":
        rpos = jnp.where(rpos >= tq, rpos - tq, rpos)
    cpos = lax.broadcasted_iota(jnp.int32, (rows, tk), 1)
    jd = qi * n_diag
    for c in range(n_diag):
        s, start = scores(jd + c)
        if mode == "moba":
            row_blk = rpos // MOBA_BLOCK
            s_sel = s + _lane_repeat(chosen_bias(jd + c), reps)
            s = jnp.where(row_blk == c, jnp.where(cpos + c * tk <= rpos, s, NEG_INF),
                          jnp.where(row_blk > c, s_sel, NEG_INF))
        else:
            s = jnp.where(cpos + c * tk <= rpos, s, NEG_INF)
        accumulate(s, start, first=(c == 0))

    def consume(s, j):
        if mode == "moba":
            s = s + _lane_repeat(chosen_bias(j), reps)
        accumulate(s, pl.multiple_of(j * tk, tk), first=False)

    def past_pair(jj, carry):
        j0 = 2 * jj
        sb_sc[...] = scores(j0 + 1)[0]
        consume(sa_sc[...], j0)
        sa_sc[...] = scores(jnp.minimum(j0 + 2, jd - 2))[0]
        consume(sb_sc[...], j0 + 1)
        return carry

    sa_sc[...] = scores(0)[0]
    lax.fori_loop(0, qi * (n_diag // 2), past_pair, 0)

    out = acc_sc[...] / l_sc[...]
    if mode == "diff":
        lam_rows = lam_ref[...]
        lam = (jnp.exp(jnp.sum(lam_rows[0:1] * lam_rows[1:2], axis=1, keepdims=True))
               - jnp.exp(jnp.sum(lam_rows[2:3] * lam_rows[3:4], axis=1, keepdims=True)) + lam_init)
        a = out[:tq] - lam * out[tq:]
        ms = jnp.mean(a * a, axis=-1, keepdims=True)
        out = a * lax.rsqrt(ms + EPS) * subg_ref[...] * (1.0 - lam_init)
    o_ref[0] = out.astype(o_ref.dtype)


ATTN_TILES = {"diff": (1024, 512), "moba": (1024, MOBA_BLOCK), "fox": (1024, 512)}


def _attention(qkv, *, mode, n_heads, q_col, k_col, v_col, slopes=None, lam_rows=None,
               subln_g=None, cum=None, lam_init=0.0):
    bsz, seq, _ = qkv.shape
    tq, tk = ATTN_TILES[mode]
    tq = min(tq, seq)
    assert tq % (2 * tk) == 0 and seq % tq == 0
    rows = 2 * tq if mode == "diff" else tq
    q_spec = pl.BlockSpec((1, tq, HEAD_DIM), lambda b, h, i: (b, i, q_col + h))
    k_spec = pl.BlockSpec((1, seq, HEAD_DIM), lambda b, h, i: (b, 0, k_col + h))
    v_spec = pl.BlockSpec((1, seq, HEAD_DIM), lambda b, h, i: (b, 0, v_col + h))
    smem = pl.BlockSpec(memory_space=pltpu.SMEM)
    scratch = [pltpu.VMEM((rows, LANES), F32), pltpu.VMEM((rows, LANES), F32),
               pltpu.VMEM((rows, HEAD_DIM), F32),
               pltpu.VMEM((rows, tk), F32), pltpu.VMEM((rows, tk), F32),
               pltpu.VMEM((seq, 2 * HEAD_DIM), BF16)]
    if mode == "diff":
        in_specs = [smem, q_spec, k_spec, v_spec,
                    pl.BlockSpec((SUBLANES, LANES), lambda b, h, i: (0, 0)),
                    pl.BlockSpec((1, HEAD_DIM), lambda b, h, i: (0, 0))]
        args = [slopes, qkv, qkv, qkv, lam_rows, subln_g]
    elif mode == "moba":
        assert tk == MOBA_BLOCK and seq // MOBA_BLOCK <= LANES
        in_specs = [smem, q_spec, k_spec, v_spec]
        args = [slopes, qkv, qkv, qkv]
        scratch += [pltpu.VMEM((LANES, HEAD_DIM), F32), pltpu.VMEM((MOBA_TOPK, tq, LANES), F32)]
    else:
        in_specs = [q_spec, k_spec, v_spec,
                    pl.BlockSpec((1, 1, seq // tk, tk), lambda b, h, i: (b, h, 0, 0))]
        args = [qkv, qkv, qkv, cum]
    return pl.pallas_call(
        functools.partial(_attn_kernel, mode=mode, tq=tq, tk=tk, seq=seq, lam_init=lam_init),
        out_shape=jax.ShapeDtypeStruct((bsz, seq, n_heads * HEAD_DIM), BF16),
        grid=(bsz, n_heads, seq // tq),
        in_specs=in_specs,
        out_specs=pl.BlockSpec((1, tq, HEAD_DIM), lambda b, h, i: (b, i, h)),
        scratch_shapes=scratch,
        compiler_params=_compiler_params(("arbitrary", "arbitrary", "arbitrary")),
        name="attn_" + mode,
    )(*args)


def _route(logits_t, bias_t):
    aff = jax.nn.sigmoid(logits_t)
    sel = aff + bias_t
    sel_rows = [sel[e:e + 1, :] for e in range(N_EXPERTS)]
    aff_rows = [aff[e:e + 1, :] for e in range(N_EXPERTS)]

    def first_argmax(vals):
        idx = jnp.zeros_like(vals[0])
        top = vals[0]
        for r in range(1, len(vals)):
            better = vals[r] > top
            idx = jnp.where(better, float(r), idx)
            top = jnp.where(better, vals[r], top)
        return idx

    group_scores = []
    for g in range(N_GROUPS):
        a, b, c, d = sel_rows[EXPERTS_PER_GROUP * g:EXPERTS_PER_GROUP * (g + 1)]
        group_scores.append(jnp.maximum(jnp.maximum(jnp.maximum(a + b, a + c), jnp.maximum(a + d, b + c)),
                                        jnp.maximum(b + d, c + d)))
    best = first_argmax(group_scores)

    def in_best_group(rows_, r):
        out = rows_[(N_GROUPS - 1) * EXPERTS_PER_GROUP + r]
        for g in range(N_GROUPS - 2, -1, -1):
            out = jnp.where(best == float(g), rows_[g * EXPERTS_PER_GROUP + r], out)
        return out

    cand_sel = [in_best_group(sel_rows, r) for r in range(EXPERTS_PER_GROUP)]
    cand_aff = [in_best_group(aff_rows, r) for r in range(EXPERTS_PER_GROUP)]
    i1 = first_argmax(cand_sel)
    i2 = first_argmax([jnp.where(i1 == float(r), -jnp.inf, cand_sel[r])
                       for r in range(EXPERTS_PER_GROUP)])

    def pick(idx):
        out = cand_aff[EXPERTS_PER_GROUP - 1]
        for r in range(EXPERTS_PER_GROUP - 2, -1, -1):
            out = jnp.where(idx == float(r), cand_aff[r], out)
        return out

    w1, w2 = pick(i1), pick(i2)
    den = w1 + w2
    w1, w2 = w1 / den, w2 / den
    gate_rows = []
    for e in range(N_EXPERTS):
        g, r = divmod(e, EXPERTS_PER_GROUP)
        in_expert = jnp.where(i1 == float(r), w1, jnp.where(i2 == float(r), w2, 0.0))
        gate_rows.append(jnp.where(best == float(g), in_expert, 0.0))
    return jnp.concatenate(gate_rows, axis=0), best


def _outproj_kernel(a_ref, b_ref, w_ref, x_ref, mod_ref, g_ref, rw_ref, rb_ref,
                    xo_ref, h_ref, gates_ref, group_ref):
    half = a_ref.shape[1]
    y = (jnp.dot(a_ref[...], w_ref[0:half, :], preferred_element_type=F32)
         + jnp.dot(b_ref[...], w_ref[half:2 * half, :], preferred_element_type=F32))
    x_new = x_ref[...] + mod_ref[0, 2:3, :] * y
    xo_ref[...] = x_new
    h = _modulated_norm(x_new, g_ref[...], mod_ref, 3, 4)
    h_ref[...] = h.astype(BF16)
    logits = jnp.dot(h, rw_ref[...], precision=HIGHEST, preferred_element_type=F32)
    gates_t, best = _route(logits.T[0:N_EXPERTS, :], rb_ref[...])
    tm = gates_t.shape[1]
    padded = jnp.concatenate(
        [gates_t, best, jnp.zeros((LANES - N_EXPERTS - 1, tm), F32)], axis=0)
    gates_ref[...] = padded.T
    group_ref[0] = best


def _outproj(parts, w_bf16, x2, mod, ffn_gain, router_w_pad, router_b_t, *, seq, tm=512):
    n_tok, d = x2.shape
    half = d // 2
    tiles_per_seq = seq // tm
    (a, ca), (b, cb) = parts
    return pl.pallas_call(
        _outproj_kernel,
        out_shape=[jax.ShapeDtypeStruct((n_tok, d), F32),
                   jax.ShapeDtypeStruct((n_tok, d), BF16),
                   jax.ShapeDtypeStruct((n_tok, LANES), F32),
                   jax.ShapeDtypeStruct((n_tok // tm, 1, tm), F32)],
        grid=(n_tok // tm,),
        in_specs=[
            pl.BlockSpec((tm, half), lambda i: (i, ca)),
            pl.BlockSpec((tm, half), lambda i: (i, cb)),
            pl.BlockSpec((d, d), lambda i: (0, 0)),
            pl.BlockSpec((tm, d), lambda i: (i, 0)),
            pl.BlockSpec((1, 6, d), lambda i: (i // tiles_per_seq, 0, 0)),
            pl.BlockSpec((1, d), lambda i: (0, 0)),
            pl.BlockSpec((d, LANES), lambda i: (0, 0)),
            pl.BlockSpec((N_EXPERTS, tm), lambda i: (0, 0)),
        ],
        out_specs=[pl.BlockSpec((tm, d), lambda i: (i, 0)),
                   pl.BlockSpec((tm, d), lambda i: (i, 0)),
                   pl.BlockSpec((tm, LANES), lambda i: (i, 0)),
                   pl.BlockSpec((1, 1, tm), lambda i: (i, 0, 0))],
        compiler_params=_compiler_params(("arbitrary",)),
        name="outproj_router",
    )(a, b, w_bf16, x2, mod, ffn_gain.reshape(1, d), router_w_pad,
      jnp.broadcast_to(router_b_t, (N_EXPERTS, tm)))


MOE_CHUNK = 256


def _split_bf16x3(x):
    hi = x.astype(BF16)
    r1 = x - hi.astype(F32)
    mid = r1.astype(BF16)
    lo = (r1 - mid.astype(F32)).astype(BF16)
    return hi, mid, lo


def _moe_kernel(h_ref, gates_ref, group_ref, w1_ref, w3_ref, w2_ref, o_ref, acc_sc):
    g = pl.program_id(0)
    gf = g.astype(F32)
    tm = h_ref.shape[0]
    n_local = w1_ref.shape[0]

    gates = gates_ref[...]
    lane = lax.broadcasted_iota(jnp.int32, (tm, LANES), 1)
    grp_col = jnp.sum(jnp.where(lane == GROUP_LANE, gates, 0.0), axis=1, keepdims=True)
    member_col = jnp.broadcast_to(jnp.where(grp_col == gf, 1.0, 0.0), (tm, LANES))
    member_row = jnp.broadcast_to(jnp.where(group_ref[0] == gf, 1.0, 0.0), (SUBLANES, tm))

    ri = lax.broadcasted_iota(jnp.int32, (tm, tm), 0)
    ci = lax.broadcasted_iota(jnp.int32, (tm, tm), 1)
    lower = jnp.where(ci < ri, 1.0, 0.0).astype(BF16)
    upper = jnp.where(ri < ci, 1.0, 0.0).astype(BF16)
    rank_col = jnp.dot(lower, member_col.astype(BF16), preferred_element_type=F32)
    rank_row = jnp.dot(member_row.astype(BF16), upper, preferred_element_type=F32)[0:1, :]
    pos_col = jnp.where(member_col > 0.0, rank_col, -1.0)
    pos_row = jnp.where(member_row[0:1, :] > 0.0, rank_row, -1.0)
    count = jnp.sum(member_row[0:1, :]).astype(jnp.int32)
    n_chunks = lax.shift_right_logical(count + (MOE_CHUNK - 1), MOE_CHUNK.bit_length() - 1)

    gates3 = jnp.concatenate(_split_bf16x3(gates), axis=1)
    acc_sc[...] = jnp.zeros_like(acc_sc)

    def chunk(k, carry):
        base = (k * MOE_CHUNK).astype(F32)
        slot_r = lax.broadcasted_iota(jnp.int32, (MOE_CHUNK, tm), 0).astype(F32) + base
        slot_c = lax.broadcasted_iota(jnp.int32, (tm, MOE_CHUNK), 1).astype(F32) + base
        gather = jnp.where(pos_row == slot_r, 1.0, 0.0).astype(BF16)
        scatter = jnp.where(_lane_repeat(pos_col, MOE_CHUNK // LANES) == slot_c,
                            1.0, 0.0).astype(BF16)
        hc = jnp.dot(gather, h_ref[...], preferred_element_type=F32).astype(BF16)
        g3 = jnp.dot(gather, gates3, preferred_element_type=F32)
        gc = g3[:, 0:LANES] + g3[:, LANES:2 * LANES] + g3[:, 2 * LANES:3 * LANES]
        clane = lax.broadcasted_iota(jnp.int32, (MOE_CHUNK, LANES), 1)
        y = None
        for el in range(n_local):
            gate_col = jnp.sum(jnp.where(clane == g * n_local + el, gc, 0.0), axis=1, keepdims=True)
            u = jnp.dot(hc, w1_ref[el], preferred_element_type=F32)
            v = jnp.dot(hc, w3_ref[el], preferred_element_type=F32)
            act = ((u * jax.nn.sigmoid(u)) * v * gate_col).astype(BF16)
            part = jnp.dot(act, w2_ref[el], preferred_element_type=F32)
            y = part if y is None else y + part
        acc_sc[...] += jnp.dot(scatter, y.astype(BF16), preferred_element_type=F32)
        return carry

    lax.fori_loop(0, n_chunks, chunk, 0)
    o_ref[0] = acc_sc[...].astype(o_ref.dtype)


def _moe(h2, gates, group_rows, w1, w3, w2, *, tm=1024):
    n_tok, d = h2.shape
    n_exp, _, d_ff = w1.shape
    tm = min(tm, n_tok)
    per_group = n_exp // N_GROUPS
    once = pl.Buffered(1)
    return pl.pallas_call(
        _moe_kernel,
        out_shape=jax.ShapeDtypeStruct((N_GROUPS, n_tok, d), BF16),
        grid=(N_GROUPS, n_tok // tm),
        in_specs=[
            pl.BlockSpec((tm, d), lambda g, i: (i, 0), pipeline_mode=once),
            pl.BlockSpec((tm, LANES), lambda g, i: (i, 0)),
            pl.BlockSpec((1, 1, tm), lambda g, i: (i, 0, 0)),
            pl.BlockSpec((per_group, d, d_ff), lambda g, i: (g, 0, 0), pipeline_mode=once),
            pl.BlockSpec((per_group, d, d_ff), lambda g, i: (g, 0, 0), pipeline_mode=once),
            pl.BlockSpec((per_group, d_ff, d), lambda g, i: (g, 0, 0), pipeline_mode=once),
        ],
        out_specs=pl.BlockSpec((1, tm, d), lambda g, i: (g, i, 0)),
        scratch_shapes=[pltpu.VMEM((tm, d), F32)],
        compiler_params=_compiler_params(("arbitrary", "arbitrary")),
        name="moe_grouped",
    )(h2, gates, group_rows.reshape(n_tok // tm, 1, tm), w1, w3, w2)


def _combine_kernel(x_ref, y_ref, mod_ref, fg_ref, o_ref, *, final_norm):
    y = y_ref[0].astype(F32)
    for g in range(1, y_ref.shape[0]):
        y = y + y_ref[g].astype(F32)
    x_new = x_ref[...] + mod_ref[0, 5:6, :] * y
    if final_norm:
        ms = jnp.mean(x_new * x_new, axis=-1, keepdims=True)
        x_new = x_new * lax.rsqrt(ms + EPS) * fg_ref[...]
    o_ref[...] = x_new


def _combine(x2, y, mod, final_gain, *, seq, final_norm, tm=512):
    n_tok, d = x2.shape
    tm = min(tm, seq)
    tiles_per_seq = seq // tm
    return pl.pallas_call(
        functools.partial(_combine_kernel, final_norm=final_norm),
        out_shape=jax.ShapeDtypeStruct((n_tok, d), F32),
        grid=(n_tok // tm,),
        in_specs=[
            pl.BlockSpec((tm, d), lambda i: (i, 0)),
            pl.BlockSpec((y.shape[0], tm, d), lambda i: (0, i, 0)),
            pl.BlockSpec((1, 6, d), lambda i: (i // tiles_per_seq, 0, 0)),
            pl.BlockSpec((1, d), lambda i: (0, 0)),
        ],
        out_specs=pl.BlockSpec((tm, d), lambda i: (i, 0)),
        compiler_params=_compiler_params(("arbitrary",)),
        name="combine_final" if final_norm else "combine",
    )(x2, y, mod, final_gain.reshape(1, d))


def _alibi_slopes(n):
    return 2.0 ** (-8.0 * np.arange(1, n + 1) / n)


def kernel(x, c, ada_w, ada_b, attn_norm_g, ffn_norm_g, w_in_even, w_out_even, lam_q1, lam_k1,
           lam_q2, lam_k2, subln_g, w_in_odd, b_forget, w_out_odd, router_w, router_b, moe_w1,
           moe_w3, moe_w2, final_norm_g):
    bsz, seq, d = x.shape
    depth = ada_w.shape[0]
    n_tok = bsz * seq
    x2 = x.reshape(n_tok, d)

    mod_all = _adaln(c, ada_w, ada_b)

    slopes = _alibi_slopes(A_HEADS + B_HEADS)
    slopes_a = jnp.asarray(slopes[0::2] * LOG2E, F32)
    slopes_b = jnp.asarray(slopes[1::2] * LOG2E, F32)
    a_width = A_HEADS * HEAD_DIM
    b_width = B_HEADS * HEAD_DIM
    c_width = C_HEADS * HEAD_DIM
    cs_even = np.ones((1, 3 * a_width + 3 * b_width), np.float32)
    cs_even[0, :a_width] = A_QK_DIM ** -0.5 * LOG2E
    cs_even[0, 3 * a_width:3 * a_width + b_width] = HEAD_DIM ** -0.5 * LOG2E
    cs_odd = np.ones((1, 3 * c_width), np.float32)
    cs_odd[0, :c_width] = HEAD_DIM ** -0.5 * LOG2E

    router_w_pad = jnp.pad(router_w, ((0, 0), (0, LANES - N_EXPERTS)))
    router_b_t = router_b.reshape(N_EXPERTS, 1)

    for l in range(depth):
        mod = mod_all[l]
        j = l // 2
        if l % 2 == 0:
            (qkv,) = _inproj(x2, attn_norm_g[l], mod, w_in_even[j].astype(BF16),
                             jnp.asarray(cs_even), seq=seq)
            qkv = qkv.reshape(bsz, seq, -1)
            lam_init = 0.8 - 0.6 * math.exp(-0.3 * l)
            lam_rows = jnp.zeros((SUBLANES, LANES), F32)
            lam_rows = lam_rows.at[0:4, 0:A_QK_DIM].set(
                jnp.stack([lam_q1[j], lam_k1[j], lam_q2[j], lam_k2[j]]).astype(F32))
            a_out = _attention(qkv, mode="diff", n_heads=A_HEADS, q_col=0, k_col=A_HEADS,
                               v_col=2 * A_HEADS, slopes=slopes_a,
                               lam_rows=lam_rows, subln_g=subln_g[j].reshape(1, HEAD_DIM),
                               lam_init=lam_init)
            b_out = _attention(qkv, mode="moba", n_heads=B_HEADS, q_col=3 * A_HEADS,
                               k_col=3 * A_HEADS + B_HEADS, v_col=3 * A_HEADS + 2 * B_HEADS,
                               slopes=slopes_b)
            parts = [(a_out.reshape(n_tok, a_width), 0), (b_out.reshape(n_tok, b_width), 0)]
            w_out = w_out_even[j]
        else:
            w_in = w_in_odd[j]
            wf = jnp.pad(w_in[:, 3 * c_width:], ((0, 0), (0, LANES - C_HEADS)))
            qkv, z = _inproj(x2, attn_norm_g[l], mod, w_in[:, :3 * c_width].astype(BF16),
                             jnp.asarray(cs_odd), wf, seq=seq)
            qkv = qkv.reshape(bsz, seq, -1)
            bf_row = jnp.pad(b_forget[j].astype(F32), (0, LANES - C_HEADS)).reshape(1, LANES)
            cum = _forget_cumsum(z.reshape(bsz, seq, LANES), bf_row)
            tk = ATTN_TILES["fox"][1]
            cum = cum[:, :C_HEADS, :].reshape(bsz, C_HEADS, seq // tk, tk)
            c_out = _attention(qkv, mode="fox", n_heads=C_HEADS, q_col=0, k_col=C_HEADS,
                               v_col=2 * C_HEADS, cum=cum)
            c_out = c_out.reshape(n_tok, c_width)
            parts = [(c_out, 0), (c_out, 1)]
            w_out = w_out_odd[j]
        x2, h2, gates, group_rows = _outproj(parts, w_out.astype(BF16), x2, mod, ffn_norm_g[l],
                                             router_w_pad, router_b_t, seq=seq)
        y = _moe(h2, gates, group_rows, moe_w1[l].astype(BF16), moe_w3[l].astype(BF16),
                 moe_w2[l].astype(BF16))
        x2 = _combine(x2, y, mod, final_norm_g, seq=seq, final_norm=(l == depth - 1))
    return x2.reshape(bsz, seq, d)
```

```python
import functools
import math

import jax
import jax.numpy as jnp
import numpy as np
from jax import lax
from jax.experimental import pallas as pl
from jax.experimental.pallas import tpu as pltpu

F32 = jnp.float32
BF16 = jnp.bfloat16
HIGHEST = lax.Precision.HIGHEST

HEAD_DIM = 128
A_HEADS = 8
A_QK_DIM = 64
B_HEADS = 8
C_HEADS = 16
MOBA_BLOCK = 256
MOBA_TOPK = 3
N_EXPERTS = 16
N_GROUPS = 4
EXPERTS_PER_GROUP = N_EXPERTS // N_GROUPS
EPS = 1e-6
NEG_INF = -1e30
LOG2E = 1.4426950408889634
GROUP_LANE = N_EXPERTS

LANES = 128
SUBLANES = 8
VMEM_LIMIT_BYTES = 56 * 1024 * 1024

NT_DIMS = (((1,), (1,)), ((), ()))


def _dot_bf16x3(a, b):
    a_hi = a.astype(BF16)
    a_lo = (a - a_hi.astype(F32)).astype(BF16)
    b_hi = b.astype(BF16)
    b_lo = (b - b_hi.astype(F32)).astype(BF16)
    return (jnp.dot(a_hi, b_hi, preferred_element_type=F32)
            + jnp.dot(a_hi, b_lo, preferred_element_type=F32)
            + jnp.dot(a_lo, b_hi, preferred_element_type=F32))


def _lane_repeat(x, n):
    return x if n == 1 else jnp.concatenate([x] * n, axis=1)


def _compiler_params(semantics):
    return pltpu.CompilerParams(dimension_semantics=semantics, vmem_limit_bytes=VMEM_LIMIT_BYTES)


def _adaln_kernel(c_ref, w_ref, b_ref, o_ref):
    c = c_ref[...]
    cond = c * jax.nn.sigmoid(c)
    o_ref[0] = jnp.dot(cond, w_ref[0], precision=HIGHEST, preferred_element_type=F32) + b_ref[0]


def _adaln(c, ada_w, ada_b, *, tn=1024):
    depth, d, n = ada_w.shape
    bsz = c.shape[0]
    rows = -(-bsz // SUBLANES) * SUBLANES
    c_pad = jnp.pad(c, ((0, rows - bsz), (0, 0)))
    out = pl.pallas_call(
        _adaln_kernel,
        out_shape=jax.ShapeDtypeStruct((depth, rows, n), F32),
        grid=(depth, n // tn),
        in_specs=[
            pl.BlockSpec((rows, d), lambda l, j: (0, 0)),
            pl.BlockSpec((1, d, tn), lambda l, j: (l, 0, j)),
            pl.BlockSpec((1, 1, tn), lambda l, j: (l, 0, j)),
        ],
        out_specs=pl.BlockSpec((1, rows, tn), lambda l, j: (l, 0, j)),
        compiler_params=_compiler_params(("arbitrary", "arbitrary")),
        name="adaln",
    )(c_pad, ada_w, ada_b.reshape(depth, 1, n))
    return out[:, :bsz].reshape(depth, bsz, 6, d)


def _modulated_norm(x, gain_row, mod_ref, shift_idx, scale_idx):
    ms = jnp.mean(x * x, axis=-1, keepdims=True)
    xn = x * lax.rsqrt(ms + EPS)
    gain = gain_row * (1.0 + mod_ref[0, scale_idx:scale_idx + 1, :])
    return xn * gain + mod_ref[0, shift_idx:shift_idx + 1, :]


def _inproj_kernel(*refs, with_forget):
    if with_forget:
        x_ref, g_ref, mod_ref, w_ref, cs_ref, wf_ref, o_ref, z_ref, h_sc = refs
    else:
        x_ref, g_ref, mod_ref, w_ref, cs_ref, o_ref, h_sc = refs

    @pl.when(pl.program_id(1) == 0)
    def _():
        h = _modulated_norm(x_ref[...], g_ref[...], mod_ref, 0, 1)
        h_sc[...] = h.astype(BF16)
        if with_forget:
            z_ref[...] = _dot_bf16x3(h, wf_ref[...])

    acc = jnp.dot(h_sc[...], w_ref[...], preferred_element_type=F32)
    o_ref[...] = (acc * cs_ref[...]).astype(o_ref.dtype)


def _inproj(x2, gain, mod, w_bf16, col_scale, wf=None, *, seq, tm=1024, tn=1024):
    n_tok, d = x2.shape
    n_out = w_bf16.shape[1]
    tiles_per_seq = seq // tm
    with_forget = wf is not None
    in_specs = [
        pl.BlockSpec((tm, d), lambda i, j: (i, 0)),
        pl.BlockSpec((1, d), lambda i, j: (0, 0)),
        pl.BlockSpec((1, 6, d), lambda i, j: (i // tiles_per_seq, 0, 0)),
        pl.BlockSpec((d, tn), lambda i, j: (0, j)),
        pl.BlockSpec((1, tn), lambda i, j: (0, j)),
    ]
    args = [x2, gain.reshape(1, d), mod, w_bf16, col_scale]
    out_shape = [jax.ShapeDtypeStruct((n_tok, n_out), BF16)]
    out_specs = [pl.BlockSpec((tm, tn), lambda i, j: (i, j))]
    if with_forget:
        in_specs.append(pl.BlockSpec((d, LANES), lambda i, j: (0, 0)))
        args.append(wf)
        out_shape.append(jax.ShapeDtypeStruct((n_tok, LANES), F32))
        out_specs.append(pl.BlockSpec((tm, LANES), lambda i, j: (i, 0)))
    return pl.pallas_call(
        functools.partial(_inproj_kernel, with_forget=with_forget),
        out_shape=out_shape,
        grid=(n_tok // tm, n_out // tn),
        in_specs=in_specs,
        out_specs=out_specs,
        scratch_shapes=[pltpu.VMEM((tm, d), BF16)],
        compiler_params=_compiler_params(("arbitrary", "arbitrary")),
        name="inproj_forget" if with_forget else "inproj",
    )(*args)


def _fcum_kernel(z_ref, bf_ref, o_ref, carry_sc):
    @pl.when(pl.program_id(1) == 0)
    def _():
        carry_sc[...] = jnp.zeros_like(carry_sc)

    z = z_ref[0] + bf_ref[...]
    log_f = jnp.minimum(z, 0.0) - jnp.log1p(jnp.exp(-jnp.abs(z)))
    tc = z.shape[0]
    row = lax.broadcasted_iota(jnp.int32, (tc, tc), 0)
    col = lax.broadcasted_iota(jnp.int32, (tc, tc), 1)
    tri = jnp.where(row >= col, 1.0, 0.0).astype(F32)
    cum = jnp.dot(tri, log_f, precision=HIGHEST, preferred_element_type=F32) + carry_sc[...]
    carry_sc[...] = cum[tc - 1:tc, :]
    o_ref[0] = cum.T


def _forget_cumsum(z, b_forget_row, *, tc=256):
    bsz, seq, _ = z.shape
    return pl.pallas_call(
        _fcum_kernel,
        out_shape=jax.ShapeDtypeStruct((bsz, LANES, seq), F32),
        grid=(bsz, seq // tc),
        in_specs=[
            pl.BlockSpec((1, tc, LANES), lambda b, t: (b, t, 0)),
            pl.BlockSpec((1, LANES), lambda b, t: (0, 0)),
        ],
        out_specs=pl.BlockSpec((1, LANES, tc), lambda b, t: (b, 0, t)),
        scratch_shapes=[pltpu.VMEM((1, LANES), F32)],
        compiler_params=_compiler_params(("arbitrary", "arbitrary")),
        name="forget_cumsum",
    )(z, b_forget_row)


def _attn_kernel(*refs, mode, tq, tk, seq, lam_init):
    if mode == "diff":
        slopes_ref, q_ref, k_ref, v_ref, lam_ref, subg_ref, o_ref = refs[:7]
        m_sc, l_sc, acc_sc, sa_sc, sb_sc, vx_sc = refs[7:]
    elif mode == "moba":
        slopes_ref, q_ref, k_ref, v_ref, o_ref = refs[:5]
        m_sc, l_sc, acc_sc, sa_sc, sb_sc, vx_sc, kmean_sc, sel_sc = refs[5:]
    else:
        q_ref, k_ref, v_ref, cum_ref, o_ref = refs[:5]
        m_sc, l_sc, acc_sc, sa_sc, sb_sc, vx_sc = refs[5:]

    head = pl.program_id(1)
    qi = pl.program_id(2)
    q0 = qi * tq
    rows = 2 * tq if mode == "diff" else tq
    reps = tk // LANES
    n_diag = tq // tk

    q = q_ref[0]
    if mode == "diff":
        lane = lax.broadcasted_iota(jnp.int32, (tq, HEAD_DIM), 1)
        zero = jnp.zeros_like(q)
        q = jnp.concatenate([jnp.where(lane < A_QK_DIM, q, zero),
                             jnp.where(lane >= A_QK_DIM, q, zero)], axis=0)

    def col_bias(j):
        if mode == "fox":
            return cum_ref[0, 0, pl.ds(j, 1), :] * (-LOG2E)
        kpos = lax.broadcasted_iota(jnp.int32, (1, tk), 1) + (j * tk - q0)
        return kpos.astype(F32) * slopes_ref[head]

    def scores(j, r0=0):
        start = pl.multiple_of(j * tk, tk)
        k = k_ref[0, pl.ds(start, tk), :]
        s = lax.dot_general(q[r0:], k, NT_DIMS, preferred_element_type=F32)
        return s + col_bias(j), start

    def chosen_bias(j, r0=0):
        jf = j.astype(F32)
        return jnp.where(sel_sc[0, r0:, :] == jf, 0.0,
                         jnp.where(sel_sc[1, r0:, :] == jf, 0.0,
                                   jnp.where(sel_sc[2, r0:, :] == jf, 0.0, NEG_INF)))

    if mode == "moba":
        n_blocks = seq // MOBA_BLOCK
        sub_per_tile = tq // MOBA_BLOCK

        @pl.when(qi == 0)
        def _():
            kmean_sc[...] = jnp.zeros_like(kmean_sc)
            for n in range(n_blocks):
                blk = k_ref[0, n * MOBA_BLOCK:(n + 1) * MOBA_BLOCK, :].astype(F32)
                kmean_sc[n:n + 1, :] = jnp.mean(blk, axis=0, keepdims=True)

        gate = lax.dot_general(q.astype(F32), kmean_sc[...], NT_DIMS, precision=HIGHEST,
                               preferred_element_type=F32)
        blk_id = lax.broadcasted_iota(jnp.int32, (tq, LANES), 1).astype(F32)
        row_sub = lax.broadcasted_iota(jnp.int32, (tq, LANES), 0) // MOBA_BLOCK
        own = (row_sub + qi * sub_per_tile).astype(F32)
        gate = jnp.where(blk_id < own, gate, NEG_INF)
        for r in range(MOBA_TOPK):
            best = jnp.max(gate, axis=1, keepdims=True)
            pick = jnp.min(jnp.where(gate == best, blk_id, float(LANES)), axis=1, keepdims=True)
            sel_sc[r] = jnp.where(pick < own, pick, -1.0)
            gate = jnp.where(blk_id == pick, -jnp.inf, gate)

    @pl.when(qi == 0)
    def _():
        vx_sc[:, 0:HEAD_DIM] = v_ref[0]
        vx_sc[:, HEAD_DIM:2 * HEAD_DIM] = jnp.ones((seq, LANES), BF16)

    def accumulate(s, start, first, r0=0):
        v = vx_sc[pl.ds(start, tk), :]
        if first:
            m_new = jnp.max(s, axis=1, keepdims=True)
            p = jnp.exp2(s - m_new)
            pv = jnp.dot(p.astype(BF16), v, preferred_element_type=F32)
            m_sc[...] = jnp.broadcast_to(m_new, (rows, LANES))
            l_sc[...] = pv[:, HEAD_DIM:2 * HEAD_DIM]
            acc_sc[...] = pv[:, 0:HEAD_DIM]
            return
        m_prev = m_sc[r0:, :]
        m_new = jnp.maximum(m_prev, jnp.max(s, axis=1, keepdims=True))
        p = jnp.exp2(s - _lane_repeat(m_new, reps))
        alpha = jnp.exp2(m_prev - m_new)
        pv = jnp.dot(p.astype(BF16), v, preferred_element_type=F32)
        l_sc[r0:, :] = alpha * l_sc[r0:, :] + pv[:, HEAD_DIM:2 * HEAD_DIM]
        acc_sc[r0:, :] = alpha * acc_sc[r0:, :] + pv[:, 0:HEAD_DIM]
        m_sc[r0:, :] = m_new

    jd = qi * n_diag
    for c in range(n_diag):
        r0 = 0 if mode == "diff" else c * tk
        s, start = scores(jd + c, r0)
        rpos = lax.broadcasted_iota(jnp.int32, (rows - r0, tk), 0) + r0
        if mode == "diff":
            rpos = jnp.where(rpos >= tq, rpos - tq, rpos)
        cpos = lax.broadcasted_iota(jnp.int32, (rows - r0, tk), 1) + c * tk
        causal = cpos <= rpos
        if mode == "moba":
            s_sel = s + _lane_repeat(chosen_bias(jd + c, r0), reps)
            s = jnp.where(rpos < (c + 1) * MOBA_BLOCK, jnp.where(causal, s, NEG_INF), s_sel)
        else:
            s = jnp.where(causal, s, NEG_INF)
        accumulate(s, start, first=(c == 0), r0=r0)

    def consume(s, j):
        if mode == "moba":
            s = s + _lane_repeat(chosen_bias(j), reps)
        accumulate(s, pl.multiple_of(j * tk, tk), first=False)

    def past_pair(jj, carry):
        j0 = 2 * jj
        sb_sc[...] = scores(j0 + 1)[0]
        consume(sa_sc[...], j0)
        sa_sc[...] = scores(jnp.minimum(j0 + 2, jd - 2))[0]
        consume(sb_sc[...], j0 + 1)
        return carry

    sa_sc[...] = scores(0)[0]
    lax.fori_loop(0, qi * (n_diag // 2), past_pair, 0)

    out = acc_sc[...] / l_sc[...]
    if mode == "diff":
        lam_rows = lam_ref[...]
        lam = (jnp.exp(jnp.sum(lam_rows[0:1] * lam_rows[1:2], axis=1, keepdims=True))
               - jnp.exp(jnp.sum(lam_rows[2:3] * lam_rows[3:4], axis=1, keepdims=True)) + lam_init)
        a = out[:tq] - lam * out[tq:]
        ms = jnp.mean(a * a, axis=-1, keepdims=True)
        out = a * lax.rsqrt(ms + EPS) * subg_ref[...] * (1.0 - lam_init)
    o_ref[0] = out.astype(o_ref.dtype)


ATTN_TILES = {"diff": (1024, 512), "moba": (1024, MOBA_BLOCK), "fox": (1024, 512)}


def _attention(qkv, *, mode, n_heads, q_col, k_col, v_col, slopes=None, lam_rows=None,
               subln_g=None, cum=None, lam_init=0.0):
    bsz, seq, _ = qkv.shape
    tq, tk = ATTN_TILES[mode]
    tq = min(tq, seq)
    assert tq % (2 * tk) == 0 and seq % tq == 0
    rows = 2 * tq if mode == "diff" else tq
    q_spec = pl.BlockSpec((1, tq, HEAD_DIM), lambda b, h, i: (b, i, q_col + h))
    k_spec = pl.BlockSpec((1, seq, HEAD_DIM), lambda b, h, i: (b, 0, k_col + h))
    v_spec = pl.BlockSpec((1, seq, HEAD_DIM), lambda b, h, i: (b, 0, v_col + h))
    smem = pl.BlockSpec(memory_space=pltpu.SMEM)
    scratch = [pltpu.VMEM((rows, LANES), F32), pltpu.VMEM((rows, LANES), F32),
               pltpu.VMEM((rows, HEAD_DIM), F32),
               pltpu.VMEM((rows, tk), F32), pltpu.VMEM((rows, tk), F32),
               pltpu.VMEM((seq, 2 * HEAD_DIM), BF16)]
    if mode == "diff":
        in_specs = [smem, q_spec, k_spec, v_spec,
                    pl.BlockSpec((SUBLANES, LANES), lambda b, h, i: (0, 0)),
                    pl.BlockSpec((1, HEAD_DIM), lambda b, h, i: (0, 0))]
        args = [slopes, qkv, qkv, qkv, lam_rows, subln_g]
    elif mode == "moba":
        assert tk == MOBA_BLOCK and seq // MOBA_BLOCK <= LANES
        in_specs = [smem, q_spec, k_spec, v_spec]
        args = [slopes, qkv, qkv, qkv]
        scratch += [pltpu.VMEM((LANES, HEAD_DIM), F32), pltpu.VMEM((MOBA_TOPK, tq, LANES), F32)]
    else:
        in_specs = [q_spec, k_spec, v_spec,
                    pl.BlockSpec((1, 1, seq // tk, tk), lambda b, h, i: (b, h, 0, 0))]
        args = [qkv, qkv, qkv, cum]
    return pl.pallas_call(
        functools.partial(_attn_kernel, mode=mode, tq=tq, tk=tk, seq=seq, lam_init=lam_init),
        out_shape=jax.ShapeDtypeStruct((bsz, seq, n_heads * HEAD_DIM), BF16),
        grid=(bsz, n_heads, seq // tq),
        in_specs=in_specs,
        out_specs=pl.BlockSpec((1, tq, HEAD_DIM), lambda b, h, i: (b, i, h)),
        scratch_shapes=scratch,
        compiler_params=_compiler_params(("arbitrary", "arbitrary", "arbitrary")),
        name="attn_" + mode,
    )(*args)


def _route(logits_t, bias_t):
    aff = jax.nn.sigmoid(logits_t)
    sel = aff + bias_t
    sel_rows = [sel[e:e + 1, :] for e in range(N_EXPERTS)]
    aff_rows = [aff[e:e + 1, :] for e in range(N_EXPERTS)]

    def first_argmax(vals):
        idx = jnp.zeros_like(vals[0])
        top = vals[0]
        for r in range(1, len(vals)):
            better = vals[r] > top
            idx = jnp.where(better, float(r), idx)
            top = jnp.where(better, vals[r], top)
        return idx

    group_scores = []
    for g in range(N_GROUPS):
        a, b, c, d = sel_rows[EXPERTS_PER_GROUP * g:EXPERTS_PER_GROUP * (g + 1)]
        group_scores.append(jnp.maximum(jnp.maximum(jnp.maximum(a + b, a + c), jnp.maximum(a + d, b + c)),
                                        jnp.maximum(b + d, c + d)))
    best = first_argmax(group_scores)

    def in_best_group(rows_, r):
        out = rows_[(N_GROUPS - 1) * EXPERTS_PER_GROUP + r]
        for g in range(N_GROUPS - 2, -1, -1):
            out = jnp.where(best == float(g), rows_[g * EXPERTS_PER_GROUP + r], out)
        return out

    cand_sel = [in_best_group(sel_rows, r) for r in range(EXPERTS_PER_GROUP)]
    cand_aff = [in_best_group(aff_rows, r) for r in range(EXPERTS_PER_GROUP)]
    i1 = first_argmax(cand_sel)
    i2 = first_argmax([jnp.where(i1 == float(r), -jnp.inf, cand_sel[r])
                       for r in range(EXPERTS_PER_GROUP)])

    def pick(idx):
        out = cand_aff[EXPERTS_PER_GROUP - 1]
        for r in range(EXPERTS_PER_GROUP - 2, -1, -1):
            out = jnp.where(idx == float(r), cand_aff[r], out)
        return out

    w1, w2 = pick(i1), pick(i2)
    den = w1 + w2
    w1, w2 = w1 / den, w2 / den
    gate_rows = []
    for e in range(N_EXPERTS):
        g, r = divmod(e, EXPERTS_PER_GROUP)
        in_expert = jnp.where(i1 == float(r), w1, jnp.where(i2 == float(r), w2, 0.0))
        gate_rows.append(jnp.where(best == float(g), in_expert, 0.0))
    return jnp.concatenate(gate_rows, axis=0), best


def _outproj_kernel(a_ref, b_ref, w_ref, x_ref, mod_ref, g_ref, rw_ref, rb_ref,
                    xo_ref, h_ref, gates_ref, group_ref):
    half = a_ref.shape[1]
    y = (jnp.dot(a_ref[...], w_ref[0:half, :], preferred_element_type=F32)
         + jnp.dot(b_ref[...], w_ref[half:2 * half, :], preferred_element_type=F32))
    x_new = x_ref[...] + mod_ref[0, 2:3, :] * y
    xo_ref[...] = x_new
    h = _modulated_norm(x_new, g_ref[...], mod_ref, 3, 4)
    h_ref[...] = h.astype(BF16)
    logits = _dot_bf16x3(h, rw_ref[...])
    gates_t, best = _route(logits.T[0:N_EXPERTS, :], rb_ref[...])
    tm = gates_t.shape[1]
    padded = jnp.concatenate(
        [gates_t, best, jnp.zeros((LANES - N_EXPERTS - 1, tm), F32)], axis=0)
    gates_ref[...] = padded.T
    group_ref[0] = best


def _outproj(parts, w_bf16, x2, mod, ffn_gain, router_w_pad, router_b_t, *, seq, tm=512):
    n_tok, d = x2.shape
    half = d // 2
    tiles_per_seq = seq // tm
    (a, ca), (b, cb) = parts
    return pl.pallas_call(
        _outproj_kernel,
        out_shape=[jax.ShapeDtypeStruct((n_tok, d), F32),
                   jax.ShapeDtypeStruct((n_tok, d), BF16),
                   jax.ShapeDtypeStruct((n_tok, LANES), F32),
                   jax.ShapeDtypeStruct((n_tok // tm, 1, tm), F32)],
        grid=(n_tok // tm,),
        in_specs=[
            pl.BlockSpec((tm, half), lambda i: (i, ca)),
            pl.BlockSpec((tm, half), lambda i: (i, cb)),
            pl.BlockSpec((d, d), lambda i: (0, 0)),
            pl.BlockSpec((tm, d), lambda i: (i, 0)),
            pl.BlockSpec((1, 6, d), lambda i: (i // tiles_per_seq, 0, 0)),
            pl.BlockSpec((1, d), lambda i: (0, 0)),
            pl.BlockSpec((d, LANES), lambda i: (0, 0)),
            pl.BlockSpec((N_EXPERTS, tm), lambda i: (0, 0)),
        ],
        out_specs=[pl.BlockSpec((tm, d), lambda i: (i, 0)),
                   pl.BlockSpec((tm, d), lambda i: (i, 0)),
                   pl.BlockSpec((tm, LANES), lambda i: (i, 0)),
                   pl.BlockSpec((1, 1, tm), lambda i: (i, 0, 0))],
        compiler_params=_compiler_params(("arbitrary",)),
        name="outproj_router",
    )(a, b, w_bf16, x2, mod, ffn_gain.reshape(1, d), router_w_pad,
      jnp.broadcast_to(router_b_t, (N_EXPERTS, tm)))


MOE_CHUNK = 256


def _split_bf16x3(x):
    hi = x.astype(BF16)
    r1 = x - hi.astype(F32)
    mid = r1.astype(BF16)
    lo = (r1 - mid.astype(F32)).astype(BF16)
    return hi, mid, lo


def _moe_kernel(h_ref, gates_ref, group_ref, w1_ref, w3_ref, w2_ref, o_ref, acc_sc):
    g = pl.program_id(0)
    gf = g.astype(F32)
    tm = h_ref.shape[0]
    n_local = w1_ref.shape[0]

    gates = gates_ref[...]
    lane = lax.broadcasted_iota(jnp.int32, (tm, LANES), 1)
    grp_col = jnp.sum(jnp.where(lane == GROUP_LANE, gates, 0.0), axis=1, keepdims=True)
    member_col = jnp.broadcast_to(jnp.where(grp_col == gf, 1.0, 0.0), (tm, LANES))
    member_row = jnp.broadcast_to(jnp.where(group_ref[0] == gf, 1.0, 0.0), (SUBLANES, tm))

    ri = lax.broadcasted_iota(jnp.int32, (tm, tm), 0)
    ci = lax.broadcasted_iota(jnp.int32, (tm, tm), 1)
    lower = jnp.where(ci < ri, 1.0, 0.0).astype(BF16)
    upper = jnp.where(ri < ci, 1.0, 0.0).astype(BF16)
    rank_col = jnp.dot(lower, member_col.astype(BF16), preferred_element_type=F32)
    rank_row = jnp.dot(member_row.astype(BF16), upper, preferred_element_type=F32)[0:1, :]
    pos_col = jnp.where(member_col > 0.0, rank_col, -1.0)
    pos_row = jnp.where(member_row[0:1, :] > 0.0, rank_row, -1.0)
    count = jnp.sum(member_row[0:1, :]).astype(jnp.int32)
    n_chunks = lax.shift_right_logical(count + (MOE_CHUNK - 1), MOE_CHUNK.bit_length() - 1)

    gates3 = jnp.concatenate(_split_bf16x3(gates), axis=1)
    acc_sc[...] = jnp.zeros_like(acc_sc)

    def chunk(k, carry):
        base = (k * MOE_CHUNK).astype(F32)
        slot_r = lax.broadcasted_iota(jnp.int32, (MOE_CHUNK, tm), 0).astype(F32) + base
        slot_c = lax.broadcasted_iota(jnp.int32, (tm, MOE_CHUNK), 1).astype(F32) + base
        gather = jnp.where(pos_row == slot_r, 1.0, 0.0).astype(BF16)
        scatter = jnp.where(_lane_repeat(pos_col, MOE_CHUNK // LANES) == slot_c,
                            1.0, 0.0).astype(BF16)
        hc = jnp.dot(gather, h_ref[...], preferred_element_type=F32).astype(BF16)
        g3 = jnp.dot(gather, gates3, preferred_element_type=F32)
        gc = g3[:, 0:LANES] + g3[:, LANES:2 * LANES] + g3[:, 2 * LANES:3 * LANES]
        clane = lax.broadcasted_iota(jnp.int32, (MOE_CHUNK, LANES), 1)
        y = None
        for el in range(n_local):
            gate_col = jnp.sum(jnp.where(clane == g * n_local + el, gc, 0.0), axis=1, keepdims=True)
            u = jnp.dot(hc, w1_ref[el], preferred_element_type=F32)
            v = jnp.dot(hc, w3_ref[el], preferred_element_type=F32)
            act = ((u * jax.nn.sigmoid(u)) * v * gate_col).astype(BF16)
            part = jnp.dot(act, w2_ref[el], preferred_element_type=F32)
            y = part if y is None else y + part
        acc_sc[...] += jnp.dot(scatter, y.astype(BF16), preferred_element_type=F32)
        return carry

    lax.fori_loop(0, n_chunks, chunk, 0)
    o_ref[0] = acc_sc[...].astype(o_ref.dtype)


def _moe(h2, gates, group_rows, w1, w3, w2, *, tm=1024):
    n_tok, d = h2.shape
    n_exp, _, d_ff = w1.shape
    tm = min(tm, n_tok)
    per_group = n_exp // N_GROUPS
    once = pl.Buffered(1)
    return pl.pallas_call(
        _moe_kernel,
        out_shape=jax.ShapeDtypeStruct((N_GROUPS, n_tok, d), BF16),
        grid=(N_GROUPS, n_tok // tm),
        in_specs=[
            pl.BlockSpec((tm, d), lambda g, i: (i, 0)),
            pl.BlockSpec((tm, LANES), lambda g, i: (i, 0)),
            pl.BlockSpec((1, 1, tm), lambda g, i: (i, 0, 0)),
            pl.BlockSpec((per_group, d, d_ff), lambda g, i: (g, 0, 0), pipeline_mode=once),
            pl.BlockSpec((per_group, d, d_ff), lambda g, i: (g, 0, 0), pipeline_mode=once),
            pl.BlockSpec((per_group, d_ff, d), lambda g, i: (g, 0, 0), pipeline_mode=once),
        ],
        out_specs=pl.BlockSpec((1, tm, d), lambda g, i: (g, i, 0)),
        scratch_shapes=[pltpu.VMEM((tm, d), F32)],
        compiler_params=_compiler_params(("arbitrary", "arbitrary")),
        name="moe_grouped",
    )(h2, gates, group_rows.reshape(n_tok // tm, 1, tm), w1, w3, w2)


def _combine_kernel(x_ref, y_ref, mod_ref, fg_ref, o_ref, *, final_norm):
    y = y_ref[0].astype(F32)
    for g in range(1, y_ref.shape[0]):
        y = y + y_ref[g].astype(F32)
    x_new = x_ref[...] + mod_ref[0, 5:6, :] * y
    if final_norm:
        ms = jnp.mean(x_new * x_new, axis=-1, keepdims=True)
        x_new = x_new * lax.rsqrt(ms + EPS) * fg_ref[...]
    o_ref[...] = x_new


def _combine(x2, y, mod, final_gain, *, seq, final_norm, tm=512):
    n_tok, d = x2.shape
    tm = min(tm, seq)
    tiles_per_seq = seq // tm
    return pl.pallas_call(
        functools.partial(_combine_kernel, final_norm=final_norm),
        out_shape=jax.ShapeDtypeStruct((n_tok, d), F32),
        grid=(n_tok // tm,),
        in_specs=[
            pl.BlockSpec((tm, d), lambda i: (i, 0)),
            pl.BlockSpec((y.shape[0], tm, d), lambda i: (0, i, 0)),
            pl.BlockSpec((1, 6, d), lambda i: (i // tiles_per_seq, 0, 0)),
            pl.BlockSpec((1, d), lambda i: (0, 0)),
        ],
        out_specs=pl.BlockSpec((tm, d), lambda i: (i, 0)),
        compiler_params=_compiler_params(("arbitrary",)),
        name="combine_final" if final_norm else "combine",
    )(x2, y, mod, final_gain.reshape(1, d))


def _alibi_slopes(n):
    return 2.0 ** (-8.0 * np.arange(1, n + 1) / n)


def kernel(x, c, ada_w, ada_b, attn_norm_g, ffn_norm_g, w_in_even, w_out_even, lam_q1, lam_k1,
           lam_q2, lam_k2, subln_g, w_in_odd, b_forget, w_out_odd, router_w, router_b, moe_w1,
           moe_w3, moe_w2, final_norm_g):
    bsz, seq, d = x.shape
    depth = ada_w.shape[0]
    n_tok = bsz * seq
    x2 = x.reshape(n_tok, d)

    mod_all = _adaln(c, ada_w, ada_b)

    slopes = _alibi_slopes(A_HEADS + B_HEADS)
    slopes_a = jnp.asarray(slopes[0::2] * LOG2E, F32)
    slopes_b = jnp.asarray(slopes[1::2] * LOG2E, F32)
    a_width = A_HEADS * HEAD_DIM
    b_width = B_HEADS * HEAD_DIM
    c_width = C_HEADS * HEAD_DIM
    cs_even = np.ones((1, 3 * a_width + 3 * b_width), np.float32)
    cs_even[0, :a_width] = A_QK_DIM ** -0.5 * LOG2E
    cs_even[0, 3 * a_width:3 * a_width + b_width] = HEAD_DIM ** -0.5 * LOG2E
    cs_odd = np.ones((1, 3 * c_width), np.float32)
    cs_odd[0, :c_width] = HEAD_DIM ** -0.5 * LOG2E

    router_w_pad = jnp.pad(router_w, ((0, 0), (0, LANES - N_EXPERTS)))
    router_b_t = router_b.reshape(N_EXPERTS, 1)

    for l in range(depth):
        mod = mod_all[l]
        j = l // 2
        if l % 2 == 0:
            (qkv,) = _inproj(x2, attn_norm_g[l], mod, w_in_even[j].astype(BF16),
                             jnp.asarray(cs_even), seq=seq)
            qkv = qkv.reshape(bsz, seq, -1)
            lam_init = 0.8 - 0.6 * math.exp(-0.3 * l)
            lam_rows = jnp.zeros((SUBLANES, LANES), F32)
            lam_rows = lam_rows.at[0:4, 0:A_QK_DIM].set(
                jnp.stack([lam_q1[j], lam_k1[j], lam_q2[j], lam_k2[j]]).astype(F32))
            a_out = _attention(qkv, mode="diff", n_heads=A_HEADS, q_col=0, k_col=A_HEADS,
                               v_col=2 * A_HEADS, slopes=slopes_a,
                               lam_rows=lam_rows, subln_g=subln_g[j].reshape(1, HEAD_DIM),
                               lam_init=lam_init)
            b_out = _attention(qkv, mode="moba", n_heads=B_HEADS, q_col=3 * A_HEADS,
                               k_col=3 * A_HEADS + B_HEADS, v_col=3 * A_HEADS + 2 * B_HEADS,
                               slopes=slopes_b)
            parts = [(a_out.reshape(n_tok, a_width), 0), (b_out.reshape(n_tok, b_width), 0)]
            w_out = w_out_even[j]
        else:
            w_in = w_in_odd[j]
            wf = jnp.pad(w_in[:, 3 * c_width:], ((0, 0), (0, LANES - C_HEADS)))
            qkv, z = _inproj(x2, attn_norm_g[l], mod, w_in[:, :3 * c_width].astype(BF16),
                             jnp.asarray(cs_odd), wf, seq=seq)
            qkv = qkv.reshape(bsz, seq, -1)
            bf_row = jnp.pad(b_forget[j].astype(F32), (0, LANES - C_HEADS)).reshape(1, LANES)
            cum = _forget_cumsum(z.reshape(bsz, seq, LANES), bf_row)
            tk = ATTN_TILES["fox"][1]
            cum = cum[:, :C_HEADS, :].reshape(bsz, C_HEADS, seq // tk, tk)
            c_out = _attention(qkv, mode="fox", n_heads=C_HEADS, q_col=0, k_col=C_HEADS,
                               v_col=2 * C_HEADS, cum=cum)
            c_out = c_out.reshape(n_tok, c_width)
            parts = [(c_out, 0), (c_out, 1)]
            w_out = w_out_odd[j]
        x2, h2, gates, group_rows = _outproj(parts, w_out.astype(BF16), x2, mod, ffn_norm_g[l],
                                             router_w_pad, router_b_t, seq=seq)
        y = _moe(h2, gates, group_rows, moe_w1[l].astype(BF16), moe_w3[l].astype(BF16),
                 moe_w2[l].astype(BF16))
        x2 = _combine(x2, y, mod, final_norm_g, seq=seq, final_norm=(l == depth - 1))
    return x2.reshape(bsz, seq, d)
```

```python
import functools
import math

import jax
import jax.numpy as jnp
import numpy as np
from jax import lax
from jax.experimental import pallas as pl
from jax.experimental.pallas import tpu as pltpu

F32 = jnp.float32
BF16 = jnp.bfloat16
HIGHEST = lax.Precision.HIGHEST

HEAD_DIM = 128
A_HEADS = 8
A_QK_DIM = 64
B_HEADS = 8
C_HEADS = 16
MOBA_BLOCK = 256
MOBA_TOPK = 3
N_EXPERTS = 16
N_GROUPS = 4
EXPERTS_PER_GROUP = N_EXPERTS // N_GROUPS
EPS = 1e-6
NEG_INF = -1e30
LOG2E = 1.4426950408889634
GROUP_LANE = N_EXPERTS

LANES = 128
SUBLANES = 8
VMEM_LIMIT_BYTES = 56 * 1024 * 1024

NT_DIMS = (((1,), (1,)), ((), ()))


def _dot_bf16x3(a, b):
    a_hi = a.astype(BF16)
    a_lo = (a - a_hi.astype(F32)).astype(BF16)
    b_hi = b.astype(BF16)
    b_lo = (b - b_hi.astype(F32)).astype(BF16)
    return (jnp.dot(a_hi, b_hi, preferred_element_type=F32)
            + jnp.dot(a_hi, b_lo, preferred_element_type=F32)
            + jnp.dot(a_lo, b_hi, preferred_element_type=F32))


def _lane_repeat(x, n):
    return x if n == 1 else jnp.concatenate([x] * n, axis=1)


def _compiler_params(semantics):
    return pltpu.CompilerParams(dimension_semantics=semantics, vmem_limit_bytes=VMEM_LIMIT_BYTES)


def _adaln_kernel(c_ref, w_ref, b_ref, o_ref):
    c = c_ref[...]
    cond = c * jax.nn.sigmoid(c)
    o_ref[0] = jnp.dot(cond, w_ref[0], precision=HIGHEST, preferred_element_type=F32) + b_ref[0]


def _adaln(c, ada_w, ada_b, *, tn=1024):
    depth, d, n = ada_w.shape
    bsz = c.shape[0]
    rows = -(-bsz // SUBLANES) * SUBLANES
    c_pad = jnp.pad(c, ((0, rows - bsz), (0, 0)))
    out = pl.pallas_call(
        _adaln_kernel,
        out_shape=jax.ShapeDtypeStruct((depth, rows, n), F32),
        grid=(depth, n // tn),
        in_specs=[
            pl.BlockSpec((rows, d), lambda l, j: (0, 0)),
            pl.BlockSpec((1, d, tn), lambda l, j: (l, 0, j)),
            pl.BlockSpec((1, 1, tn), lambda l, j: (l, 0, j)),
        ],
        out_specs=pl.BlockSpec((1, rows, tn), lambda l, j: (l, 0, j)),
        compiler_params=_compiler_params(("arbitrary", "arbitrary")),
        name="adaln",
    )(c_pad, ada_w, ada_b.reshape(depth, 1, n))
    return out[:, :bsz].reshape(depth, bsz, 6, d)


def _modulated_norm(x, gain_row, mod_ref, shift_idx, scale_idx):
    ms = jnp.mean(x * x, axis=-1, keepdims=True)
    xn = x * lax.rsqrt(ms + EPS)
    gain = gain_row * (1.0 + mod_ref[0, scale_idx:scale_idx + 1, :])
    return xn * gain + mod_ref[0, shift_idx:shift_idx + 1, :]


def _inproj_kernel(*refs, with_forget):
    if with_forget:
        x_ref, g_ref, mod_ref, w_ref, cs_ref, wf_ref, o_ref, z_ref, h_sc = refs
    else:
        x_ref, g_ref, mod_ref, w_ref, cs_ref, o_ref, h_sc = refs

    @pl.when(pl.program_id(1) == 0)
    def _():
        h = _modulated_norm(x_ref[...], g_ref[...], mod_ref, 0, 1)
        h_sc[...] = h.astype(BF16)
        if with_forget:
            z_ref[...] = _dot_bf16x3(h, wf_ref[...])

    acc = jnp.dot(h_sc[...], w_ref[...], preferred_element_type=F32)
    o_ref[...] = (acc * cs_ref[...]).astype(o_ref.dtype)


def _inproj(x2, gain, mod, w_bf16, col_scale, wf=None, *, seq, tm=1024, tn=1024):
    n_tok, d = x2.shape
    n_out = col_scale.shape[1]
    tiles_per_seq = seq // tm
    with_forget = wf is not None
    in_specs = [
        pl.BlockSpec((tm, d), lambda i, j: (i, 0)),
        pl.BlockSpec((1, d), lambda i, j: (0, 0)),
        pl.BlockSpec((1, 6, d), lambda i, j: (i // tiles_per_seq, 0, 0)),
        pl.BlockSpec((d, tn), lambda i, j: (0, j)),
        pl.BlockSpec((1, tn), lambda i, j: (0, j)),
    ]
    args = [x2, gain.reshape(1, d), mod, w_bf16, col_scale]
    out_shape = [jax.ShapeDtypeStruct((n_tok, n_out), BF16)]
    out_specs = [pl.BlockSpec((tm, tn), lambda i, j: (i, j))]
    if with_forget:
        in_specs.append(pl.BlockSpec((d, LANES), lambda i, j: (0, 0)))
        args.append(wf)
        out_shape.append(jax.ShapeDtypeStruct((n_tok, LANES), F32))
        out_specs.append(pl.BlockSpec((tm, LANES), lambda i, j: (i, 0)))
    return pl.pallas_call(
        functools.partial(_inproj_kernel, with_forget=with_forget),
        out_shape=out_shape,
        grid=(n_tok // tm, n_out // tn),
        in_specs=in_specs,
        out_specs=out_specs,
        scratch_shapes=[pltpu.VMEM((tm, d), BF16)],
        compiler_params=_compiler_params(("arbitrary", "arbitrary")),
        name="inproj_forget" if with_forget else "inproj",
    )(*args)


def _fcum_kernel(z_ref, bf_ref, o_ref, carry_sc):
    @pl.when(pl.program_id(1) == 0)
    def _():
        carry_sc[...] = jnp.zeros_like(carry_sc)

    z = z_ref[0] + bf_ref[...]
    log_f = jnp.minimum(z, 0.0) - jnp.log1p(jnp.exp(-jnp.abs(z)))
    tc = z.shape[0]
    row = lax.broadcasted_iota(jnp.int32, (tc, tc), 0)
    col = lax.broadcasted_iota(jnp.int32, (tc, tc), 1)
    tri = jnp.where(row >= col, 1.0, 0.0).astype(F32)
    cum = jnp.dot(tri, log_f, precision=HIGHEST, preferred_element_type=F32) + carry_sc[...]
    carry_sc[...] = cum[tc - 1:tc, :]
    o_ref[0] = cum.T


def _forget_cumsum(z, b_forget_row, *, tc=256):
    bsz, seq, _ = z.shape
    return pl.pallas_call(
        _fcum_kernel,
        out_shape=jax.ShapeDtypeStruct((bsz, LANES, seq), F32),
        grid=(bsz, seq // tc),
        in_specs=[
            pl.BlockSpec((1, tc, LANES), lambda b, t: (b, t, 0)),
            pl.BlockSpec((1, LANES), lambda b, t: (0, 0)),
        ],
        out_specs=pl.BlockSpec((1, LANES, tc), lambda b, t: (b, 0, t)),
        scratch_shapes=[pltpu.VMEM((1, LANES), F32)],
        compiler_params=_compiler_params(("arbitrary", "arbitrary")),
        name="forget_cumsum",
    )(z, b_forget_row)


def _attn_kernel(*refs, mode, tq, tk, seq, lam_init):
    if mode == "diff":
        slopes_ref, q_ref, k_ref, v_ref, lam_ref, subg_ref, o_ref = refs[:7]
        m_sc, l_sc, acc_sc, sa_sc, sb_sc, vx_sc = refs[7:]
    elif mode == "moba":
        slopes_ref, q_ref, k_ref, v_ref, o_ref = refs[:5]
        m_sc, l_sc, acc_sc, sa_sc, sb_sc, vx_sc, kmean_sc, sel_sc = refs[5:]
    else:
        q_ref, k_ref, v_ref, cum_ref, o_ref = refs[:5]
        m_sc, l_sc, acc_sc, sa_sc, sb_sc, vx_sc = refs[5:]

    head = pl.program_id(1)
    qi = pl.program_id(2)
    q0 = qi * tq
    rows = 2 * tq if mode == "diff" else tq
    reps = tk // LANES
    n_diag = tq // tk

    q = q_ref[0]
    if mode == "diff":
        lane = lax.broadcasted_iota(jnp.int32, (tq, HEAD_DIM), 1)
        zero = jnp.zeros_like(q)
        q = jnp.concatenate([jnp.where(lane < A_QK_DIM, q, zero),
                             jnp.where(lane >= A_QK_DIM, q, zero)], axis=0)

    def col_bias(j):
        if mode == "fox":
            return cum_ref[0, 0, pl.ds(j, 1), :] * (-LOG2E)
        kpos = lax.broadcasted_iota(jnp.int32, (1, tk), 1) + (j * tk - q0)
        return kpos.astype(F32) * slopes_ref[head]

    def scores(j, r0=0):
        start = pl.multiple_of(j * tk, tk)
        k = k_ref[0, pl.ds(start, tk), :]
        s = lax.dot_general(q[r0:], k, NT_DIMS, preferred_element_type=F32)
        return s + col_bias(j), start

    def chosen_bias(j, r0=0):
        jf = j.astype(F32)
        return jnp.where(sel_sc[0, r0:, :] == jf, 0.0,
                         jnp.where(sel_sc[1, r0:, :] == jf, 0.0,
                                   jnp.where(sel_sc[2, r0:, :] == jf, 0.0, NEG_INF)))

    if mode == "moba":
        n_blocks = seq // MOBA_BLOCK
        sub_per_tile = tq // MOBA_BLOCK

        @pl.when(qi == 0)
        def _():
            kmean_sc[...] = jnp.zeros_like(kmean_sc)
            for n in range(n_blocks):
                blk = k_ref[0, n * MOBA_BLOCK:(n + 1) * MOBA_BLOCK, :].astype(F32)
                kmean_sc[n:n + 1, :] = jnp.mean(blk, axis=0, keepdims=True)

        gate = lax.dot_general(q.astype(F32), kmean_sc[...], NT_DIMS, precision=HIGHEST,
                               preferred_element_type=F32)
        blk_id = lax.broadcasted_iota(jnp.int32, (tq, LANES), 1).astype(F32)
        row_sub = lax.broadcasted_iota(jnp.int32, (tq, LANES), 0) // MOBA_BLOCK
        own = (row_sub + qi * sub_per_tile).astype(F32)
        gate = jnp.where(blk_id < own, gate, NEG_INF)
        for r in range(MOBA_TOPK):
            best = jnp.max(gate, axis=1, keepdims=True)
            pick = jnp.min(jnp.where(gate == best, blk_id, float(LANES)), axis=1, keepdims=True)
            sel_sc[r] = jnp.where(pick < own, pick, -1.0)
            gate = jnp.where(blk_id == pick, -jnp.inf, gate)

    @pl.when(qi == 0)
    def _():
        vx_sc[:, 0:HEAD_DIM] = v_ref[0]
        vx_sc[:, HEAD_DIM:2 * HEAD_DIM] = jnp.ones((seq, LANES), BF16)

    def accumulate(s, start, first, r0=0):
        v = vx_sc[pl.ds(start, tk), :]
        if first:
            m_new = jnp.max(s, axis=1, keepdims=True)
            p = jnp.exp2(s - m_new)
            pv = jnp.dot(p.astype(BF16), v, preferred_element_type=F32)
            m_sc[...] = jnp.broadcast_to(m_new, (rows, LANES))
            l_sc[...] = pv[:, HEAD_DIM:2 * HEAD_DIM]
            acc_sc[...] = pv[:, 0:HEAD_DIM]
            return
        m_prev = m_sc[r0:, :]
        m_new = jnp.maximum(m_prev, jnp.max(s, axis=1, keepdims=True))
        p = jnp.exp2(s - _lane_repeat(m_new, reps))
        alpha = jnp.exp2(m_prev - m_new)
        pv = jnp.dot(p.astype(BF16), v, preferred_element_type=F32)
        l_sc[r0:, :] = alpha * l_sc[r0:, :] + pv[:, HEAD_DIM:2 * HEAD_DIM]
        acc_sc[r0:, :] = alpha * acc_sc[r0:, :] + pv[:, 0:HEAD_DIM]
        m_sc[r0:, :] = m_new

    jd = qi * n_diag
    for c in range(n_diag):
        r0 = 0 if mode == "diff" else c * tk
        s, start = scores(jd + c, r0)
        rpos = lax.broadcasted_iota(jnp.int32, (rows - r0, tk), 0) + r0
        if mode == "diff":
            rpos = jnp.where(rpos >= tq, rpos - tq, rpos)
        cpos = lax.broadcasted_iota(jnp.int32, (rows - r0, tk), 1) + c * tk
        causal = cpos <= rpos
        if mode == "moba":
            s_sel = s + _lane_repeat(chosen_bias(jd + c, r0), reps)
            s = jnp.where(rpos < (c + 1) * MOBA_BLOCK, jnp.where(causal, s, NEG_INF), s_sel)
        else:
            s = jnp.where(causal, s, NEG_INF)
        accumulate(s, start, first=(c == 0), r0=r0)

    def consume(s, j):
        if mode == "moba":
            s = s + _lane_repeat(chosen_bias(j), reps)
        accumulate(s, pl.multiple_of(j * tk, tk), first=False)

    def past_pair(jj, carry):
        j0 = 2 * jj
        sb_sc[...] = scores(j0 + 1)[0]
        consume(sa_sc[...], j0)
        sa_sc[...] = scores(jnp.minimum(j0 + 2, jd - 2))[0]
        consume(sb_sc[...], j0 + 1)
        return carry

    sa_sc[...] = scores(0)[0]
    lax.fori_loop(0, qi * (n_diag // 2), past_pair, 0)

    out = acc_sc[...] / l_sc[...]
    if mode == "diff":
        lam_rows = lam_ref[...]
        lam = (jnp.exp(jnp.sum(lam_rows[0:1] * lam_rows[1:2], axis=1, keepdims=True))
               - jnp.exp(jnp.sum(lam_rows[2:3] * lam_rows[3:4], axis=1, keepdims=True)) + lam_init)
        a = out[:tq] - lam * out[tq:]
        ms = jnp.mean(a * a, axis=-1, keepdims=True)
        out = a * lax.rsqrt(ms + EPS) * subg_ref[...] * (1.0 - lam_init)
    o_ref[0] = out.astype(o_ref.dtype)


ATTN_TILES = {"diff": (1024, 512), "moba": (1024, MOBA_BLOCK), "fox": (1024, 512)}


def _attention(qkv, *, mode, n_heads, q_col, k_col, v_col, slopes=None, lam_rows=None,
               subln_g=None, cum=None, lam_init=0.0):
    bsz, seq, _ = qkv.shape
    tq, tk = ATTN_TILES[mode]
    tq = min(tq, seq)
    assert tq % (2 * tk) == 0 and seq % tq == 0
    rows = 2 * tq if mode == "diff" else tq
    q_spec = pl.BlockSpec((1, tq, HEAD_DIM), lambda b, h, i: (b, i, q_col + h))
    k_spec = pl.BlockSpec((1, seq, HEAD_DIM), lambda b, h, i: (b, 0, k_col + h))
    v_spec = pl.BlockSpec((1, seq, HEAD_DIM), lambda b, h, i: (b, 0, v_col + h))
    smem = pl.BlockSpec(memory_space=pltpu.SMEM)
    scratch = [pltpu.VMEM((rows, LANES), F32), pltpu.VMEM((rows, LANES), F32),
               pltpu.VMEM((rows, HEAD_DIM), F32),
               pltpu.VMEM((rows, tk), F32), pltpu.VMEM((rows, tk), F32),
               pltpu.VMEM((seq, 2 * HEAD_DIM), BF16)]
    if mode == "diff":
        in_specs = [smem, q_spec, k_spec, v_spec,
                    pl.BlockSpec((SUBLANES, LANES), lambda b, h, i: (0, 0)),
                    pl.BlockSpec((1, HEAD_DIM), lambda b, h, i: (0, 0))]
        args = [slopes, qkv, qkv, qkv, lam_rows, subln_g]
    elif mode == "moba":
        assert tk == MOBA_BLOCK and seq // MOBA_BLOCK <= LANES
        in_specs = [smem, q_spec, k_spec, v_spec]
        args = [slopes, qkv, qkv, qkv]
        scratch += [pltpu.VMEM((LANES, HEAD_DIM), F32), pltpu.VMEM((MOBA_TOPK, tq, LANES), F32)]
    else:
        in_specs = [q_spec, k_spec, v_spec,
                    pl.BlockSpec((1, 1, seq // tk, tk), lambda b, h, i: (b, h, 0, 0))]
        args = [qkv, qkv, qkv, cum]
    return pl.pallas_call(
        functools.partial(_attn_kernel, mode=mode, tq=tq, tk=tk, seq=seq, lam_init=lam_init),
        out_shape=jax.ShapeDtypeStruct((bsz, seq, n_heads * HEAD_DIM), BF16),
        grid=(bsz, n_heads, seq // tq),
        in_specs=in_specs,
        out_specs=pl.BlockSpec((1, tq, HEAD_DIM), lambda b, h, i: (b, i, h)),
        scratch_shapes=scratch,
        compiler_params=_compiler_params(("arbitrary", "arbitrary", "arbitrary")),
        name="attn_" + mode,
    )(*args)


def _route(logits_t, bias_t):
    aff = jax.nn.sigmoid(logits_t)
    sel = aff + bias_t
    sel_rows = [sel[e:e + 1, :] for e in range(N_EXPERTS)]
    aff_rows = [aff[e:e + 1, :] for e in range(N_EXPERTS)]

    def first_argmax(vals):
        idx = jnp.zeros_like(vals[0])
        top = vals[0]
        for r in range(1, len(vals)):
            better = vals[r] > top
            idx = jnp.where(better, float(r), idx)
            top = jnp.where(better, vals[r], top)
        return idx

    group_scores = []
    for g in range(N_GROUPS):
        a, b, c, d = sel_rows[EXPERTS_PER_GROUP * g:EXPERTS_PER_GROUP * (g + 1)]
        group_scores.append(jnp.maximum(jnp.maximum(jnp.maximum(a + b, a + c), jnp.maximum(a + d, b + c)),
                                        jnp.maximum(b + d, c + d)))
    best = first_argmax(group_scores)

    def in_best_group(rows_, r):
        out = rows_[(N_GROUPS - 1) * EXPERTS_PER_GROUP + r]
        for g in range(N_GROUPS - 2, -1, -1):
            out = jnp.where(best == float(g), rows_[g * EXPERTS_PER_GROUP + r], out)
        return out

    cand_sel = [in_best_group(sel_rows, r) for r in range(EXPERTS_PER_GROUP)]
    cand_aff = [in_best_group(aff_rows, r) for r in range(EXPERTS_PER_GROUP)]
    i1 = first_argmax(cand_sel)
    i2 = first_argmax([jnp.where(i1 == float(r), -jnp.inf, cand_sel[r])
                       for r in range(EXPERTS_PER_GROUP)])

    def pick(idx):
        out = cand_aff[EXPERTS_PER_GROUP - 1]
        for r in range(EXPERTS_PER_GROUP - 2, -1, -1):
            out = jnp.where(idx == float(r), cand_aff[r], out)
        return out

    w1, w2 = pick(i1), pick(i2)
    den = w1 + w2
    w1, w2 = w1 / den, w2 / den
    gate_rows = []
    for e in range(N_EXPERTS):
        g, r = divmod(e, EXPERTS_PER_GROUP)
        in_expert = jnp.where(i1 == float(r), w1, jnp.where(i2 == float(r), w2, 0.0))
        gate_rows.append(jnp.where(best == float(g), in_expert, 0.0))
    return jnp.concatenate(gate_rows, axis=0), best


def _outproj_kernel(a_ref, b_ref, w_ref, x_ref, mod_ref, g_ref, rw_ref, rb_ref,
                    xo_ref, h_ref, gates_ref, group_ref):
    half = a_ref.shape[1]
    y = (jnp.dot(a_ref[...], w_ref[0:half, :], preferred_element_type=F32)
         + jnp.dot(b_ref[...], w_ref[half:2 * half, :], preferred_element_type=F32))
    x_new = x_ref[...] + mod_ref[0, 2:3, :] * y
    xo_ref[...] = x_new
    h = _modulated_norm(x_new, g_ref[...], mod_ref, 3, 4)
    h_ref[...] = h.astype(BF16)
    logits = _dot_bf16x3(h, rw_ref[...])
    gates_t, best = _route(logits.T[0:N_EXPERTS, :], rb_ref[...])
    tm = gates_t.shape[1]
    padded = jnp.concatenate(
        [gates_t, best, jnp.zeros((LANES - N_EXPERTS - 1, tm), F32)], axis=0)
    gates_ref[...] = padded.T
    group_ref[0] = best


def _outproj(parts, w_bf16, x2, mod, ffn_gain, router_w_pad, router_b_t, *, seq, tm=512):
    n_tok, d = x2.shape
    half = d // 2
    tiles_per_seq = seq // tm
    (a, ca), (b, cb) = parts
    return pl.pallas_call(
        _outproj_kernel,
        out_shape=[jax.ShapeDtypeStruct((n_tok, d), F32),
                   jax.ShapeDtypeStruct((n_tok, d), BF16),
                   jax.ShapeDtypeStruct((n_tok, LANES), F32),
                   jax.ShapeDtypeStruct((n_tok // tm, 1, tm), F32)],
        grid=(n_tok // tm,),
        in_specs=[
            pl.BlockSpec((tm, half), lambda i: (i, ca)),
            pl.BlockSpec((tm, half), lambda i: (i, cb)),
            pl.BlockSpec((d, d), lambda i: (0, 0)),
            pl.BlockSpec((tm, d), lambda i: (i, 0)),
            pl.BlockSpec((1, 6, d), lambda i: (i // tiles_per_seq, 0, 0)),
            pl.BlockSpec((1, d), lambda i: (0, 0)),
            pl.BlockSpec((d, LANES), lambda i: (0, 0)),
            pl.BlockSpec((N_EXPERTS, tm), lambda i: (0, 0)),
        ],
        out_specs=[pl.BlockSpec((tm, d), lambda i: (i, 0)),
                   pl.BlockSpec((tm, d), lambda i: (i, 0)),
                   pl.BlockSpec((tm, LANES), lambda i: (i, 0)),
                   pl.BlockSpec((1, 1, tm), lambda i: (i, 0, 0))],
        compiler_params=_compiler_params(("arbitrary",)),
        name="outproj_router",
    )(a, b, w_bf16, x2, mod, ffn_gain.reshape(1, d), router_w_pad,
      jnp.broadcast_to(router_b_t, (N_EXPERTS, tm)))


MOE_CHUNK = 256
MOE_TAIL_CHUNK = 128


def _split_bf16x3(x):
    hi = x.astype(BF16)
    r1 = x - hi.astype(F32)
    mid = r1.astype(BF16)
    lo = (r1 - mid.astype(F32)).astype(BF16)
    return hi, mid, lo


def _moe_kernel(h_ref, gates_ref, group_ref, w1_ref, w3_ref, w2_ref, o_ref, acc_sc):
    g = pl.program_id(0)
    gf = g.astype(F32)
    tm = h_ref.shape[0]
    n_local = w1_ref.shape[0]

    gates = gates_ref[...]
    lane = lax.broadcasted_iota(jnp.int32, (tm, LANES), 1)
    grp_col = jnp.sum(jnp.where(lane == GROUP_LANE, gates, 0.0), axis=1, keepdims=True)
    member_col = jnp.broadcast_to(jnp.where(grp_col == gf, 1.0, 0.0), (tm, LANES))
    member_row = jnp.broadcast_to(jnp.where(group_ref[0] == gf, 1.0, 0.0), (SUBLANES, tm))

    ri = lax.broadcasted_iota(jnp.int32, (tm, tm), 0)
    ci = lax.broadcasted_iota(jnp.int32, (tm, tm), 1)
    lower = jnp.where(ci < ri, 1.0, 0.0).astype(BF16)
    upper = jnp.where(ri < ci, 1.0, 0.0).astype(BF16)
    rank_col = jnp.dot(lower, member_col.astype(BF16), preferred_element_type=F32)
    rank_row = jnp.dot(member_row.astype(BF16), upper, preferred_element_type=F32)[0:1, :]
    pos_col = jnp.where(member_col > 0.0, rank_col, -1.0)
    pos_row = jnp.where(member_row[0:1, :] > 0.0, rank_row, -1.0)
    count = jnp.sum(member_row[0:1, :]).astype(jnp.int32)
    n_tail = lax.shift_right_logical(jnp.maximum(count - MOE_CHUNK, 0) + (MOE_TAIL_CHUNK - 1),
                                     MOE_TAIL_CHUNK.bit_length() - 1)

    gates3 = jnp.concatenate(_split_bf16x3(gates), axis=1)
    acc_sc[...] = jnp.zeros_like(acc_sc)

    def run_chunk(size, base):
        slot_r = lax.broadcasted_iota(jnp.int32, (size, tm), 0).astype(F32) + base
        slot_c = lax.broadcasted_iota(jnp.int32, (tm, size), 1).astype(F32) + base
        gather = jnp.where(pos_row == slot_r, 1.0, 0.0).astype(BF16)
        scatter = jnp.where(_lane_repeat(pos_col, size // LANES) == slot_c,
                            1.0, 0.0).astype(BF16)
        hc = jnp.dot(gather, h_ref[...], preferred_element_type=F32).astype(BF16)
        g3 = jnp.dot(gather, gates3, preferred_element_type=F32)
        gc = g3[:, 0:LANES] + g3[:, LANES:2 * LANES] + g3[:, 2 * LANES:3 * LANES]
        clane = lax.broadcasted_iota(jnp.int32, (size, LANES), 1)
        y = None
        for el in range(n_local):
            gate_col = jnp.sum(jnp.where(clane == g * n_local + el, gc, 0.0), axis=1, keepdims=True)
            u = jnp.dot(hc, w1_ref[el], preferred_element_type=F32)
            v = jnp.dot(hc, w3_ref[el], preferred_element_type=F32)
            act = ((u * jax.nn.sigmoid(u)) * v * gate_col).astype(BF16)
            part = jnp.dot(act, w2_ref[el], preferred_element_type=F32)
            y = part if y is None else y + part
        acc_sc[...] += jnp.dot(scatter, y.astype(BF16), preferred_element_type=F32)

    @pl.when(count > 0)
    def _():
        run_chunk(MOE_CHUNK, 0.0)

    def tail_chunk(k, carry):
        run_chunk(MOE_TAIL_CHUNK, (MOE_CHUNK + k * MOE_TAIL_CHUNK).astype(F32))
        return carry

    lax.fori_loop(0, n_tail, tail_chunk, 0)
    o_ref[0] = acc_sc[...].astype(o_ref.dtype)


def _moe(h2, gates, group_rows, w1, w3, w2, *, tm=1024):
    n_tok, d = h2.shape
    n_exp, _, d_ff = w1.shape
    tm = min(tm, n_tok)
    per_group = n_exp // N_GROUPS
    once = pl.Buffered(1)
    return pl.pallas_call(
        _moe_kernel,
        out_shape=jax.ShapeDtypeStruct((N_GROUPS, n_tok, d), BF16),
        grid=(N_GROUPS, n_tok // tm),
        in_specs=[
            pl.BlockSpec((tm, d), lambda g, i: (i, 0)),
            pl.BlockSpec((tm, LANES), lambda g, i: (i, 0)),
            pl.BlockSpec((1, 1, tm), lambda g, i: (i, 0, 0)),
            pl.BlockSpec((per_group, d, d_ff), lambda g, i: (g, 0, 0), pipeline_mode=once),
            pl.BlockSpec((per_group, d, d_ff), lambda g, i: (g, 0, 0), pipeline_mode=once),
            pl.BlockSpec((per_group, d_ff, d), lambda g, i: (g, 0, 0), pipeline_mode=once),
        ],
        out_specs=pl.BlockSpec((1, tm, d), lambda g, i: (g, i, 0)),
        scratch_shapes=[pltpu.VMEM((tm, d), F32)],
        compiler_params=_compiler_params(("arbitrary", "arbitrary")),
        name="moe_grouped",
    )(h2, gates, group_rows.reshape(n_tok // tm, 1, tm), w1, w3, w2)


def _combine_kernel(x_ref, y_ref, mod_ref, fg_ref, o_ref, *, final_norm):
    y = y_ref[0].astype(F32)
    for g in range(1, y_ref.shape[0]):
        y = y + y_ref[g].astype(F32)
    x_new = x_ref[...] + mod_ref[0, 5:6, :] * y
    if final_norm:
        ms = jnp.mean(x_new * x_new, axis=-1, keepdims=True)
        x_new = x_new * lax.rsqrt(ms + EPS) * fg_ref[...]
    o_ref[...] = x_new


def _combine(x2, y, mod, final_gain, *, seq, final_norm, tm=512):
    n_tok, d = x2.shape
    tm = min(tm, seq)
    tiles_per_seq = seq // tm
    return pl.pallas_call(
        functools.partial(_combine_kernel, final_norm=final_norm),
        out_shape=jax.ShapeDtypeStruct((n_tok, d), F32),
        grid=(n_tok // tm,),
        in_specs=[
            pl.BlockSpec((tm, d), lambda i: (i, 0)),
            pl.BlockSpec((y.shape[0], tm, d), lambda i: (0, i, 0)),
            pl.BlockSpec((1, 6, d), lambda i: (i // tiles_per_seq, 0, 0)),
            pl.BlockSpec((1, d), lambda i: (0, 0)),
        ],
        out_specs=pl.BlockSpec((tm, d), lambda i: (i, 0)),
        compiler_params=_compiler_params(("arbitrary",)),
        name="combine_final" if final_norm else "combine",
    )(x2, y, mod, final_gain.reshape(1, d))


def _alibi_slopes(n):
    return 2.0 ** (-8.0 * np.arange(1, n + 1) / n)


def kernel(x, c, ada_w, ada_b, attn_norm_g, ffn_norm_g, w_in_even, w_out_even, lam_q1, lam_k1,
           lam_q2, lam_k2, subln_g, w_in_odd, b_forget, w_out_odd, router_w, router_b, moe_w1,
           moe_w3, moe_w2, final_norm_g):
    bsz, seq, d = x.shape
    depth = ada_w.shape[0]
    n_tok = bsz * seq
    x2 = x.reshape(n_tok, d)

    mod_all = _adaln(c, ada_w, ada_b)

    slopes = _alibi_slopes(A_HEADS + B_HEADS)
    slopes_a = jnp.asarray(slopes[0::2] * LOG2E, F32)
    slopes_b = jnp.asarray(slopes[1::2] * LOG2E, F32)
    a_width = A_HEADS * HEAD_DIM
    b_width = B_HEADS * HEAD_DIM
    c_width = C_HEADS * HEAD_DIM
    cs_even = np.ones((1, 3 * a_width + 3 * b_width), np.float32)
    cs_even[0, :a_width] = A_QK_DIM ** -0.5 * LOG2E
    cs_even[0, 3 * a_width:3 * a_width + b_width] = HEAD_DIM ** -0.5 * LOG2E
    cs_odd = np.ones((1, 3 * c_width), np.float32)
    cs_odd[0, :c_width] = HEAD_DIM ** -0.5 * LOG2E

    router_w_pad = jnp.pad(router_w, ((0, 0), (0, LANES - N_EXPERTS)))
    router_b_t = router_b.reshape(N_EXPERTS, 1)

    for l in range(depth):
        mod = mod_all[l]
        j = l // 2
        if l % 2 == 0:
            (qkv,) = _inproj(x2, attn_norm_g[l], mod, w_in_even[j].astype(BF16),
                             jnp.asarray(cs_even), seq=seq)
            qkv = qkv.reshape(bsz, seq, -1)
            lam_init = 0.8 - 0.6 * math.exp(-0.3 * l)
            lam_rows = jnp.zeros((SUBLANES, LANES), F32)
            lam_rows = lam_rows.at[0:4, 0:A_QK_DIM].set(
                jnp.stack([lam_q1[j], lam_k1[j], lam_q2[j], lam_k2[j]]).astype(F32))
            a_out = _attention(qkv, mode="diff", n_heads=A_HEADS, q_col=0, k_col=A_HEADS,
                               v_col=2 * A_HEADS, slopes=slopes_a,
                               lam_rows=lam_rows, subln_g=subln_g[j].reshape(1, HEAD_DIM),
                               lam_init=lam_init)
            b_out = _attention(qkv, mode="moba", n_heads=B_HEADS, q_col=3 * A_HEADS,
                               k_col=3 * A_HEADS + B_HEADS, v_col=3 * A_HEADS + 2 * B_HEADS,
                               slopes=slopes_b)
            parts = [(a_out.reshape(n_tok, a_width), 0), (b_out.reshape(n_tok, b_width), 0)]
            w_out = w_out_even[j]
        else:
            w_in = w_in_odd[j]
            wf = jnp.pad(w_in[:, 3 * c_width:], ((0, 0), (0, LANES - C_HEADS)))
            qkv, z = _inproj(x2, attn_norm_g[l], mod, w_in.astype(BF16), jnp.asarray(cs_odd), wf,
                             seq=seq)
            qkv = qkv.reshape(bsz, seq, -1)
            bf_row = jnp.pad(b_forget[j].astype(F32), (0, LANES - C_HEADS)).reshape(1, LANES)
            cum = _forget_cumsum(z.reshape(bsz, seq, LANES), bf_row)
            tk = ATTN_TILES["fox"][1]
            cum = cum[:, :C_HEADS, :].reshape(bsz, C_HEADS, seq // tk, tk)
            c_out = _attention(qkv, mode="fox", n_heads=C_HEADS, q_col=0, k_col=C_HEADS,
                               v_col=2 * C_HEADS, cum=cum)
            c_out = c_out.reshape(n_tok, c_width)
            parts = [(c_out, 0), (c_out, 1)]
            w_out = w_out_odd[j]
        x2, h2, gates, group_rows = _outproj(parts, w_out.astype(BF16), x2, mod, ffn_norm_g[l],
                                             router_w_pad, router_b_t, seq=seq)
        y = _moe(h2, gates, group_rows, moe_w1[l].astype(BF16), moe_w3[l].astype(BF16),
                 moe_w2[l].astype(BF16))
        x2 = _combine(x2, y, mod, final_norm_g, seq=seq, final_norm=(l == depth - 1))
    return x2.reshape(bsz, seq, d)
```

```python
import functools
import math

import jax
import jax.numpy as jnp
import numpy as np
from jax import lax
from jax.experimental import pallas as pl
from jax.experimental.pallas import tpu as pltpu

F32 = jnp.float32
BF16 = jnp.bfloat16
HIGHEST = lax.Precision.HIGHEST

HEAD_DIM = 128
A_HEADS = 8
A_QK_DIM = 64
B_HEADS = 8
C_HEADS = 16
MOBA_BLOCK = 256
MOBA_TOPK = 3
N_EXPERTS = 16
N_GROUPS = 4
EXPERTS_PER_GROUP = N_EXPERTS // N_GROUPS
EPS = 1e-6
NEG_INF = -1e30
LOG2E = 1.4426950408889634
GROUP_LANE = N_EXPERTS

LANES = 128
SUBLANES = 8
VMEM_LIMIT_BYTES = 56 * 1024 * 1024

NT_DIMS = (((1,), (1,)), ((), ()))


def _dot_bf16x3(a, b):
    a_hi = a.astype(BF16)
    a_lo = (a - a_hi.astype(F32)).astype(BF16)
    b_hi = b.astype(BF16)
    b_lo = (b - b_hi.astype(F32)).astype(BF16)
    return (jnp.dot(a_hi, b_hi, preferred_element_type=F32)
            + jnp.dot(a_hi, b_lo, preferred_element_type=F32)
            + jnp.dot(a_lo, b_hi, preferred_element_type=F32))


def _lane_repeat(x, n):
    return x if n == 1 else jnp.concatenate([x] * n, axis=1)


def _compiler_params(semantics):
    return pltpu.CompilerParams(dimension_semantics=semantics, vmem_limit_bytes=VMEM_LIMIT_BYTES)


def _adaln_kernel(c_ref, w_ref, b_ref, o_ref):
    c = c_ref[...]
    cond = c * jax.nn.sigmoid(c)
    o_ref[0] = jnp.dot(cond, w_ref[0], precision=HIGHEST, preferred_element_type=F32) + b_ref[0]


def _adaln(c, ada_w, ada_b, *, tn=1024):
    depth, d, n = ada_w.shape
    bsz = c.shape[0]
    rows = -(-bsz // SUBLANES) * SUBLANES
    c_pad = jnp.pad(c, ((0, rows - bsz), (0, 0)))
    out = pl.pallas_call(
        _adaln_kernel,
        out_shape=jax.ShapeDtypeStruct((depth, rows, n), F32),
        grid=(depth, n // tn),
        in_specs=[
            pl.BlockSpec((rows, d), lambda l, j: (0, 0)),
            pl.BlockSpec((1, d, tn), lambda l, j: (l, 0, j)),
            pl.BlockSpec((1, 1, tn), lambda l, j: (l, 0, j)),
        ],
        out_specs=pl.BlockSpec((1, rows, tn), lambda l, j: (l, 0, j)),
        compiler_params=_compiler_params(("arbitrary", "arbitrary")),
        name="adaln",
    )(c_pad, ada_w, ada_b.reshape(depth, 1, n))
    return out[:, :bsz].reshape(depth, bsz, 6, d)


def _modulated_norm(x, gain_row, mod_ref, shift_idx, scale_idx):
    ms = jnp.mean(x * x, axis=-1, keepdims=True)
    xn = x * lax.rsqrt(ms + EPS)
    gain = gain_row * (1.0 + mod_ref[0, scale_idx:scale_idx + 1, :])
    return xn * gain + mod_ref[0, shift_idx:shift_idx + 1, :]


def _inproj_kernel(*refs, with_forget):
    if with_forget:
        x_ref, g_ref, mod_ref, w_ref, cs_ref, wf_ref, o_ref, z_ref, h_sc = refs
    else:
        x_ref, g_ref, mod_ref, w_ref, cs_ref, o_ref, h_sc = refs

    @pl.when(pl.program_id(1) == 0)
    def _():
        h = _modulated_norm(x_ref[...], g_ref[...], mod_ref, 0, 1)
        h_sc[...] = h.astype(BF16)
        if with_forget:
            z_ref[...] = _dot_bf16x3(h, wf_ref[...])

    acc = jnp.dot(h_sc[...], w_ref[...], preferred_element_type=F32)
    o_ref[...] = (acc * cs_ref[...]).astype(o_ref.dtype)


def _inproj(x2, gain, mod, w_bf16, col_scale, wf=None, *, seq, tm=1024, tn=1024):
    n_tok, d = x2.shape
    n_out = col_scale.shape[1]
    tiles_per_seq = seq // tm
    with_forget = wf is not None
    in_specs = [
        pl.BlockSpec((tm, d), lambda i, j: (i, 0)),
        pl.BlockSpec((1, d), lambda i, j: (0, 0)),
        pl.BlockSpec((1, 6, d), lambda i, j: (i // tiles_per_seq, 0, 0)),
        pl.BlockSpec((d, tn), lambda i, j: (0, j)),
        pl.BlockSpec((1, tn), lambda i, j: (0, j)),
    ]
    args = [x2, gain.reshape(1, d), mod, w_bf16, col_scale]
    out_shape = [jax.ShapeDtypeStruct((n_tok, n_out), BF16)]
    out_specs = [pl.BlockSpec((tm, tn), lambda i, j: (i, j))]
    if with_forget:
        in_specs.append(pl.BlockSpec((d, LANES), lambda i, j: (0, 0)))
        args.append(wf)
        out_shape.append(jax.ShapeDtypeStruct((n_tok, LANES), F32))
        out_specs.append(pl.BlockSpec((tm, LANES), lambda i, j: (i, 0)))
    return pl.pallas_call(
        functools.partial(_inproj_kernel, with_forget=with_forget),
        out_shape=out_shape,
        grid=(n_tok // tm, n_out // tn),
        in_specs=in_specs,
        out_specs=out_specs,
        scratch_shapes=[pltpu.VMEM((tm, d), BF16)],
        compiler_params=_compiler_params(("arbitrary", "arbitrary")),
        name="inproj_forget" if with_forget else "inproj",
    )(*args)


def _fcum_kernel(z_ref, bf_ref, o_ref, carry_sc):
    @pl.when(pl.program_id(1) == 0)
    def _():
        carry_sc[...] = jnp.zeros_like(carry_sc)

    z = z_ref[0] + bf_ref[...]
    log_f = jnp.minimum(z, 0.0) - jnp.log1p(jnp.exp(-jnp.abs(z)))
    tc = z.shape[0]
    row = lax.broadcasted_iota(jnp.int32, (tc, tc), 0)
    col = lax.broadcasted_iota(jnp.int32, (tc, tc), 1)
    tri = jnp.where(row >= col, 1.0, 0.0).astype(F32)
    cum = jnp.dot(tri, log_f, precision=HIGHEST, preferred_element_type=F32) + carry_sc[...]
    carry_sc[...] = cum[tc - 1:tc, :]
    o_ref[0] = cum.T


def _forget_cumsum(z, b_forget_row, *, tc=256):
    bsz, seq, _ = z.shape
    return pl.pallas_call(
        _fcum_kernel,
        out_shape=jax.ShapeDtypeStruct((bsz, LANES, seq), F32),
        grid=(bsz, seq // tc),
        in_specs=[
            pl.BlockSpec((1, tc, LANES), lambda b, t: (b, t, 0)),
            pl.BlockSpec((1, LANES), lambda b, t: (0, 0)),
        ],
        out_specs=pl.BlockSpec((1, LANES, tc), lambda b, t: (b, 0, t)),
        scratch_shapes=[pltpu.VMEM((1, LANES), F32)],
        compiler_params=_compiler_params(("arbitrary", "arbitrary")),
        name="forget_cumsum",
    )(z, b_forget_row)


def _attn_kernel(*refs, mode, tq, tk, seq, lam_init):
    if mode == "diff":
        slopes_ref, q_ref, k_ref, v_ref, lam_ref, subg_ref, o_ref = refs[:7]
        m_sc, l_sc, acc_sc, sa_sc, sb_sc, vx_sc = refs[7:]
    elif mode == "moba":
        slopes_ref, q_ref, k_ref, v_ref, o_ref = refs[:5]
        m_sc, l_sc, acc_sc, sa_sc, sb_sc, vx_sc, kmean_sc, sel_sc, kx_sc = refs[5:]
    else:
        q_ref, k_ref, v_ref, cum_ref, o_ref = refs[:5]
        m_sc, l_sc, acc_sc, sa_sc, sb_sc, vx_sc = refs[5:]

    head = pl.program_id(1)
    qi = pl.program_id(2)
    q0 = qi * tq
    rows = 2 * tq if mode == "diff" else tq
    reps = tk // LANES
    n_diag = tq // tk

    q = q_ref[0]
    if mode == "diff":
        lane = lax.broadcasted_iota(jnp.int32, (tq, HEAD_DIM), 1)
        zero = jnp.zeros_like(q)
        q = jnp.concatenate([jnp.where(lane < A_QK_DIM, q, zero),
                             jnp.where(lane >= A_QK_DIM, q, zero)], axis=0)

    def col_bias(j):
        if mode == "fox":
            return cum_ref[0, 0, pl.ds(j, 1), :] * (-LOG2E)
        kpos = lax.broadcasted_iota(jnp.int32, (1, tk), 1) + (j * tk - q0)
        return kpos.astype(F32) * slopes_ref[head]

    def scores(j, r0=0):
        start = pl.multiple_of(j * tk, tk)
        k = k_ref[0, pl.ds(start, tk), :]
        s = lax.dot_general(q[r0:], k, NT_DIMS, preferred_element_type=F32)
        return s + col_bias(j), start

    def chosen_bias(j, r0=0):
        jf = j.astype(F32)
        return jnp.where(sel_sc[0, r0:, :] == jf, 0.0,
                         jnp.where(sel_sc[1, r0:, :] == jf, 0.0,
                                   jnp.where(sel_sc[2, r0:, :] == jf, 0.0, NEG_INF)))

    if mode == "moba":
        n_blocks = seq // MOBA_BLOCK
        sub_per_tile = tq // MOBA_BLOCK

        @pl.when(qi == 0)
        def _():
            kmean_sc[...] = jnp.zeros_like(kmean_sc)
            for n in range(n_blocks):
                blk = k_ref[0, n * MOBA_BLOCK:(n + 1) * MOBA_BLOCK, :].astype(F32)
                kmean_sc[n:n + 1, :] = jnp.mean(blk, axis=0, keepdims=True)

        gate = lax.dot_general(q.astype(F32), kmean_sc[...], NT_DIMS, precision=HIGHEST,
                               preferred_element_type=F32)
        blk_id = lax.broadcasted_iota(jnp.int32, (tq, LANES), 1).astype(F32)
        row_sub = lax.broadcasted_iota(jnp.int32, (tq, LANES), 0) // MOBA_BLOCK
        own = (row_sub + qi * sub_per_tile).astype(F32)
        gate = jnp.where(blk_id < own, gate, NEG_INF)
        block_bias = jnp.full((tq, LANES), NEG_INF, F32)
        for r in range(MOBA_TOPK):
            best = jnp.max(gate, axis=1, keepdims=True)
            pick = jnp.min(jnp.where(gate == best, blk_id, float(LANES)), axis=1, keepdims=True)
            pick = jnp.where(pick < own, pick, -1.0)
            sel_sc[r] = jnp.broadcast_to(pick, (tq, LANES))
            block_bias = jnp.where(blk_id == pick, 0.0, block_bias)
            gate = jnp.where(blk_id == pick, -jnp.inf, gate)

        slope = slopes_ref[head]

        @pl.when(qi == 0)
        def _():
            kx_sc[:, 0:HEAD_DIM] = k_ref[0]
            lane_k = lax.broadcasted_iota(jnp.int32, (seq, LANES), 1)
            kpos = lax.broadcasted_iota(jnp.int32, (seq, LANES), 0)
            hi, mid, lo = _split_bf16x3(kpos.astype(F32) * slope)
            aux = jnp.where(lane_k == kpos // MOBA_BLOCK, 1.0, 0.0)
            aux = jnp.where(lane_k == n_blocks, hi.astype(F32), aux)
            aux = jnp.where(lane_k == n_blocks + 1, mid.astype(F32), aux)
            aux = jnp.where(lane_k == n_blocks + 2, lo.astype(F32), aux)
            aux = jnp.where((lane_k >= n_blocks + 3) & (lane_k < n_blocks + 6), 1.0, aux)
            kx_sc[:, HEAD_DIM:2 * HEAD_DIM] = aux.astype(BF16)

        lane_q = lax.broadcasted_iota(jnp.int32, (tq, LANES), 1)
        off = jnp.full((tq, LANES), 0.0, F32) - slope * q0.astype(F32)
        hi, mid, lo = _split_bf16x3(off)
        aux_q = jnp.where(lane_q < n_blocks, block_bias, 0.0)
        aux_q = jnp.where((lane_q >= n_blocks) & (lane_q < n_blocks + 3), 1.0, aux_q)
        aux_q = jnp.where(lane_q == n_blocks + 3, hi.astype(F32), aux_q)
        aux_q = jnp.where(lane_q == n_blocks + 4, mid.astype(F32), aux_q)
        aux_q = jnp.where(lane_q == n_blocks + 5, lo.astype(F32), aux_q)
        q_wide = jnp.concatenate([q, aux_q.astype(BF16)], axis=1)

    def past_scores(j):
        if mode != "moba":
            return scores(j)[0]
        start = pl.multiple_of(j * tk, tk)
        return lax.dot_general(q_wide, kx_sc[pl.ds(start, tk), :], NT_DIMS,
                               preferred_element_type=F32)

    @pl.when(qi == 0)
    def _():
        vx_sc[:, 0:HEAD_DIM] = v_ref[0]
        vx_sc[:, HEAD_DIM:2 * HEAD_DIM] = jnp.ones((seq, LANES), BF16)

    def accumulate(s, start, first, r0=0):
        v = vx_sc[pl.ds(start, tk), :]
        if first:
            m_new = jnp.max(s, axis=1, keepdims=True)
            p = jnp.exp2(s - m_new)
            pv = jnp.dot(p.astype(BF16), v, preferred_element_type=F32)
            m_sc[...] = jnp.broadcast_to(m_new, (rows, LANES))
            l_sc[...] = pv[:, HEAD_DIM:2 * HEAD_DIM]
            acc_sc[...] = pv[:, 0:HEAD_DIM]
            return
        m_prev = m_sc[r0:, :]
        m_new = jnp.maximum(m_prev, jnp.max(s, axis=1, keepdims=True))
        p = jnp.exp2(s - _lane_repeat(m_new, reps))
        alpha = jnp.exp2(m_prev - m_new)
        pv = jnp.dot(p.astype(BF16), v, preferred_element_type=F32)
        l_sc[r0:, :] = alpha * l_sc[r0:, :] + pv[:, HEAD_DIM:2 * HEAD_DIM]
        acc_sc[r0:, :] = alpha * acc_sc[r0:, :] + pv[:, 0:HEAD_DIM]
        m_sc[r0:, :] = m_new

    jd = qi * n_diag
    for c in range(n_diag):
        r0 = 0 if mode == "diff" else c * tk
        s, start = scores(jd + c, r0)
        rpos = lax.broadcasted_iota(jnp.int32, (rows - r0, tk), 0) + r0
        if mode == "diff":
            rpos = jnp.where(rpos >= tq, rpos - tq, rpos)
        cpos = lax.broadcasted_iota(jnp.int32, (rows - r0, tk), 1) + c * tk
        causal = cpos <= rpos
        if mode == "moba":
            s_sel = s + _lane_repeat(chosen_bias(jd + c, r0), reps)
            s = jnp.where(rpos < (c + 1) * MOBA_BLOCK, jnp.where(causal, s, NEG_INF), s_sel)
        else:
            s = jnp.where(causal, s, NEG_INF)
        accumulate(s, start, first=(c == 0), r0=r0)

    def consume(s, j):
        accumulate(s, pl.multiple_of(j * tk, tk), first=False)

    def past_pair(jj, carry):
        j0 = 2 * jj
        sb_sc[...] = past_scores(j0 + 1)
        consume(sa_sc[...], j0)
        sa_sc[...] = past_scores(jnp.minimum(j0 + 2, jd - 2))
        consume(sb_sc[...], j0 + 1)
        return carry

    sa_sc[...] = past_scores(0)
    lax.fori_loop(0, qi * (n_diag // 2), past_pair, 0)

    out = acc_sc[...] / l_sc[...]
    if mode == "diff":
        lam_rows = lam_ref[...]
        lam = (jnp.exp(jnp.sum(lam_rows[0:1] * lam_rows[1:2], axis=1, keepdims=True))
               - jnp.exp(jnp.sum(lam_rows[2:3] * lam_rows[3:4], axis=1, keepdims=True)) + lam_init)
        a = out[:tq] - lam * out[tq:]
        ms = jnp.mean(a * a, axis=-1, keepdims=True)
        out = a * lax.rsqrt(ms + EPS) * subg_ref[...] * (1.0 - lam_init)
    o_ref[0] = out.astype(o_ref.dtype)


ATTN_TILES = {"diff": (1024, 512), "moba": (1024, MOBA_BLOCK), "fox": (1024, 512)}


def _attention(qkv, *, mode, n_heads, q_col, k_col, v_col, slopes=None, lam_rows=None,
               subln_g=None, cum=None, lam_init=0.0):
    bsz, seq, _ = qkv.shape
    tq, tk = ATTN_TILES[mode]
    tq = min(tq, seq)
    assert tq % (2 * tk) == 0 and seq % tq == 0
    rows = 2 * tq if mode == "diff" else tq
    q_spec = pl.BlockSpec((1, tq, HEAD_DIM), lambda b, h, i: (b, i, q_col + h))
    k_spec = pl.BlockSpec((1, seq, HEAD_DIM), lambda b, h, i: (b, 0, k_col + h))
    v_spec = pl.BlockSpec((1, seq, HEAD_DIM), lambda b, h, i: (b, 0, v_col + h))
    smem = pl.BlockSpec(memory_space=pltpu.SMEM)
    scratch = [pltpu.VMEM((rows, LANES), F32), pltpu.VMEM((rows, LANES), F32),
               pltpu.VMEM((rows, HEAD_DIM), F32),
               pltpu.VMEM((rows, tk), F32), pltpu.VMEM((rows, tk), F32),
               pltpu.VMEM((seq, 2 * HEAD_DIM), BF16)]
    if mode == "diff":
        in_specs = [smem, q_spec, k_spec, v_spec,
                    pl.BlockSpec((SUBLANES, LANES), lambda b, h, i: (0, 0)),
                    pl.BlockSpec((1, HEAD_DIM), lambda b, h, i: (0, 0))]
        args = [slopes, qkv, qkv, qkv, lam_rows, subln_g]
    elif mode == "moba":
        assert tk == MOBA_BLOCK and seq // MOBA_BLOCK <= LANES
        in_specs = [smem, q_spec, k_spec, v_spec]
        args = [slopes, qkv, qkv, qkv]
        scratch += [pltpu.VMEM((LANES, HEAD_DIM), F32), pltpu.VMEM((MOBA_TOPK, tq, LANES), F32),
                    pltpu.VMEM((seq, 2 * HEAD_DIM), BF16)]
    else:
        in_specs = [q_spec, k_spec, v_spec,
                    pl.BlockSpec((1, 1, seq // tk, tk), lambda b, h, i: (b, h, 0, 0))]
        args = [qkv, qkv, qkv, cum]
    return pl.pallas_call(
        functools.partial(_attn_kernel, mode=mode, tq=tq, tk=tk, seq=seq, lam_init=lam_init),
        out_shape=jax.ShapeDtypeStruct((bsz, seq, n_heads * HEAD_DIM), BF16),
        grid=(bsz, n_heads, seq // tq),
        in_specs=in_specs,
        out_specs=pl.BlockSpec((1, tq, HEAD_DIM), lambda b, h, i: (b, i, h)),
        scratch_shapes=scratch,
        compiler_params=_compiler_params(("arbitrary", "arbitrary", "arbitrary")),
        name="attn_" + mode,
    )(*args)


def _route(logits_t, bias_t):
    aff = jax.nn.sigmoid(logits_t)
    sel = aff + bias_t
    sel_rows = [sel[e:e + 1, :] for e in range(N_EXPERTS)]
    aff_rows = [aff[e:e + 1, :] for e in range(N_EXPERTS)]

    def first_argmax(vals):
        idx = jnp.zeros_like(vals[0])
        top = vals[0]
        for r in range(1, len(vals)):
            better = vals[r] > top
            idx = jnp.where(better, float(r), idx)
            top = jnp.where(better, vals[r], top)
        return idx

    group_scores = []
    for g in range(N_GROUPS):
        a, b, c, d = sel_rows[EXPERTS_PER_GROUP * g:EXPERTS_PER_GROUP * (g + 1)]
        group_scores.append(jnp.maximum(jnp.maximum(jnp.maximum(a + b, a + c), jnp.maximum(a + d, b + c)),
                                        jnp.maximum(b + d, c + d)))
    best = first_argmax(group_scores)

    def in_best_group(rows_, r):
        out = rows_[(N_GROUPS - 1) * EXPERTS_PER_GROUP + r]
        for g in range(N_GROUPS - 2, -1, -1):
            out = jnp.where(best == float(g), rows_[g * EXPERTS_PER_GROUP + r], out)
        return out

    cand_sel = [in_best_group(sel_rows, r) for r in range(EXPERTS_PER_GROUP)]
    cand_aff = [in_best_group(aff_rows, r) for r in range(EXPERTS_PER_GROUP)]
    i1 = first_argmax(cand_sel)
    i2 = first_argmax([jnp.where(i1 == float(r), -jnp.inf, cand_sel[r])
                       for r in range(EXPERTS_PER_GROUP)])

    def pick(idx):
        out = cand_aff[EXPERTS_PER_GROUP - 1]
        for r in range(EXPERTS_PER_GROUP - 2, -1, -1):
            out = jnp.where(idx == float(r), cand_aff[r], out)
        return out

    w1, w2 = pick(i1), pick(i2)
    den = w1 + w2
    w1, w2 = w1 / den, w2 / den
    gate_rows = []
    for e in range(N_EXPERTS):
        g, r = divmod(e, EXPERTS_PER_GROUP)
        in_expert = jnp.where(i1 == float(r), w1, jnp.where(i2 == float(r), w2, 0.0))
        gate_rows.append(jnp.where(best == float(g), in_expert, 0.0))
    return jnp.concatenate(gate_rows, axis=0), best


def _outproj_kernel(a_ref, b_ref, w_ref, x_ref, mod_ref, g_ref, rw_ref, rb_ref,
                    xo_ref, h_ref, gates_ref, group_ref):
    half = a_ref.shape[1]
    y = (jnp.dot(a_ref[...], w_ref[0:half, :], preferred_element_type=F32)
         + jnp.dot(b_ref[...], w_ref[half:2 * half, :], preferred_element_type=F32))
    x_new = x_ref[...] + mod_ref[0, 2:3, :] * y
    xo_ref[...] = x_new
    h = _modulated_norm(x_new, g_ref[...], mod_ref, 3, 4)
    h_ref[...] = h.astype(BF16)
    logits = _dot_bf16x3(h, rw_ref[...])
    gates_t, best = _route(logits.T[0:N_EXPERTS, :], rb_ref[...])
    tm = gates_t.shape[1]
    padded = jnp.concatenate(
        [gates_t, best, jnp.zeros((LANES - N_EXPERTS - 1, tm), F32)], axis=0)
    gates_ref[...] = padded.T
    group_ref[0] = best


def _outproj(parts, w_bf16, x2, mod, ffn_gain, router_w_pad, router_b_t, *, seq, tm=512):
    n_tok, d = x2.shape
    half = d // 2
    tiles_per_seq = seq // tm
    (a, ca), (b, cb) = parts
    return pl.pallas_call(
        _outproj_kernel,
        out_shape=[jax.ShapeDtypeStruct((n_tok, d), F32),
                   jax.ShapeDtypeStruct((n_tok, d), BF16),
                   jax.ShapeDtypeStruct((n_tok, LANES), F32),
                   jax.ShapeDtypeStruct((n_tok // tm, 1, tm), F32)],
        grid=(n_tok // tm,),
        in_specs=[
            pl.BlockSpec((tm, half), lambda i: (i, ca)),
            pl.BlockSpec((tm, half), lambda i: (i, cb)),
            pl.BlockSpec((d, d), lambda i: (0, 0)),
            pl.BlockSpec((tm, d), lambda i: (i, 0)),
            pl.BlockSpec((1, 6, d), lambda i: (i // tiles_per_seq, 0, 0)),
            pl.BlockSpec((1, d), lambda i: (0, 0)),
            pl.BlockSpec((d, LANES), lambda i: (0, 0)),
            pl.BlockSpec((N_EXPERTS, tm), lambda i: (0, 0)),
        ],
        out_specs=[pl.BlockSpec((tm, d), lambda i: (i, 0)),
                   pl.BlockSpec((tm, d), lambda i: (i, 0)),
                   pl.BlockSpec((tm, LANES), lambda i: (i, 0)),
                   pl.BlockSpec((1, 1, tm), lambda i: (i, 0, 0))],
        compiler_params=_compiler_params(("arbitrary",)),
        name="outproj_router",
    )(a, b, w_bf16, x2, mod, ffn_gain.reshape(1, d), router_w_pad,
      jnp.broadcast_to(router_b_t, (N_EXPERTS, tm)))


MOE_CHUNK = 256
MOE_TAIL_CHUNK = 128


def _split_bf16x3(x):
    hi = x.astype(BF16)
    r1 = x - hi.astype(F32)
    mid = r1.astype(BF16)
    lo = (r1 - mid.astype(F32)).astype(BF16)
    return hi, mid, lo


def _moe_kernel(h_ref, gates_ref, group_ref, w1_ref, w3_ref, w2_ref, o_ref, acc_sc):
    g = pl.program_id(0)
    gf = g.astype(F32)
    tm = h_ref.shape[0]
    n_local = w1_ref.shape[0]

    gates = gates_ref[...]
    lane = lax.broadcasted_iota(jnp.int32, (tm, LANES), 1)
    grp_col = jnp.sum(jnp.where(lane == GROUP_LANE, gates, 0.0), axis=1, keepdims=True)
    member_col = jnp.broadcast_to(jnp.where(grp_col == gf, 1.0, 0.0), (tm, LANES))
    member_row = jnp.broadcast_to(jnp.where(group_ref[0] == gf, 1.0, 0.0), (SUBLANES, tm))

    ri = lax.broadcasted_iota(jnp.int32, (tm, tm), 0)
    ci = lax.broadcasted_iota(jnp.int32, (tm, tm), 1)
    lower = jnp.where(ci < ri, 1.0, 0.0).astype(BF16)
    upper = jnp.where(ri < ci, 1.0, 0.0).astype(BF16)
    rank_col = jnp.dot(lower, member_col.astype(BF16), preferred_element_type=F32)
    rank_row = jnp.dot(member_row.astype(BF16), upper, preferred_element_type=F32)[0:1, :]
    pos_col = jnp.where(member_col > 0.0, rank_col, -1.0)
    pos_row = jnp.where(member_row[0:1, :] > 0.0, rank_row, -1.0)
    count = jnp.sum(member_row[0:1, :]).astype(jnp.int32)
    n_tail = lax.shift_right_logical(jnp.maximum(count - MOE_CHUNK, 0) + (MOE_TAIL_CHUNK - 1),
                                     MOE_TAIL_CHUNK.bit_length() - 1)

    gates3 = jnp.concatenate(_split_bf16x3(gates), axis=1)
    acc_sc[...] = jnp.zeros_like(acc_sc)

    def run_chunk(size, base):
        slot_r = lax.broadcasted_iota(jnp.int32, (size, tm), 0).astype(F32) + base
        slot_c = lax.broadcasted_iota(jnp.int32, (tm, size), 1).astype(F32) + base
        gather = jnp.where(pos_row == slot_r, 1.0, 0.0).astype(BF16)
        scatter = jnp.where(_lane_repeat(pos_col, size // LANES) == slot_c,
                            1.0, 0.0).astype(BF16)
        hc = jnp.dot(gather, h_ref[...], preferred_element_type=F32).astype(BF16)
        g3 = jnp.dot(gather, gates3, preferred_element_type=F32)
        gc = g3[:, 0:LANES] + g3[:, LANES:2 * LANES] + g3[:, 2 * LANES:3 * LANES]
        clane = lax.broadcasted_iota(jnp.int32, (size, LANES), 1)
        y = None
        for el in range(n_local):
            gate_col = jnp.sum(jnp.where(clane == g * n_local + el, gc, 0.0), axis=1, keepdims=True)
            u = jnp.dot(hc, w1_ref[el], preferred_element_type=F32)
            v = jnp.dot(hc, w3_ref[el], preferred_element_type=F32)
            act = ((u * jax.nn.sigmoid(u)) * v * gate_col).astype(BF16)
            part = jnp.dot(act, w2_ref[el], preferred_element_type=F32)
            y = part if y is None else y + part
        acc_sc[...] += jnp.dot(scatter, y.astype(BF16), preferred_element_type=F32)

    @pl.when(count > 0)
    def _():
        run_chunk(MOE_CHUNK, 0.0)

    def tail_chunk(k, carry):
        run_chunk(MOE_TAIL_CHUNK, (MOE_CHUNK + k * MOE_TAIL_CHUNK).astype(F32))
        return carry

    lax.fori_loop(0, n_tail, tail_chunk, 0)
    o_ref[0] = acc_sc[...].astype(o_ref.dtype)


def _moe(h2, gates, group_rows, w1, w3, w2, *, tm=1024):
    n_tok, d = h2.shape
    n_exp, _, d_ff = w1.shape
    tm = min(tm, n_tok)
    per_group = n_exp // N_GROUPS
    once = pl.Buffered(1)
    return pl.pallas_call(
        _moe_kernel,
        out_shape=jax.ShapeDtypeStruct((N_GROUPS, n_tok, d), BF16),
        grid=(N_GROUPS, n_tok // tm),
        in_specs=[
            pl.BlockSpec((tm, d), lambda g, i: (i, 0)),
            pl.BlockSpec((tm, LANES), lambda g, i: (i, 0)),
            pl.BlockSpec((1, 1, tm), lambda g, i: (i, 0, 0)),
            pl.BlockSpec((per_group, d, d_ff), lambda g, i: (g, 0, 0), pipeline_mode=once),
            pl.BlockSpec((per_group, d, d_ff), lambda g, i: (g, 0, 0), pipeline_mode=once),
            pl.BlockSpec((per_group, d_ff, d), lambda g, i: (g, 0, 0), pipeline_mode=once),
        ],
        out_specs=pl.BlockSpec((1, tm, d), lambda g, i: (g, i, 0)),
        scratch_shapes=[pltpu.VMEM((tm, d), F32)],
        compiler_params=_compiler_params(("arbitrary", "arbitrary")),
        name="moe_grouped",
    )(h2, gates, group_rows.reshape(n_tok // tm, 1, tm), w1, w3, w2)


def _combine_kernel(x_ref, y_ref, mod_ref, fg_ref, o_ref, *, final_norm):
    y = y_ref[0].astype(F32)
    for g in range(1, y_ref.shape[0]):
        y = y + y_ref[g].astype(F32)
    x_new = x_ref[...] + mod_ref[0, 5:6, :] * y
    if final_norm:
        ms = jnp.mean(x_new * x_new, axis=-1, keepdims=True)
        x_new = x_new * lax.rsqrt(ms + EPS) * fg_ref[...]
    o_ref[...] = x_new


def _combine(x2, y, mod, final_gain, *, seq, final_norm, tm=512):
    n_tok, d = x2.shape
    tm = min(tm, seq)
    tiles_per_seq = seq // tm
    return pl.pallas_call(
        functools.partial(_combine_kernel, final_norm=final_norm),
        out_shape=jax.ShapeDtypeStruct((n_tok, d), F32),
        grid=(n_tok // tm,),
        in_specs=[
            pl.BlockSpec((tm, d), lambda i: (i, 0)),
            pl.BlockSpec((y.shape[0], tm, d), lambda i: (0, i, 0)),
            pl.BlockSpec((1, 6, d), lambda i: (i // tiles_per_seq, 0, 0)),
            pl.BlockSpec((1, d), lambda i: (0, 0)),
        ],
        out_specs=pl.BlockSpec((tm, d), lambda i: (i, 0)),
        compiler_params=_compiler_params(("arbitrary",)),
        name="combine_final" if final_norm else "combine",
    )(x2, y, mod, final_gain.reshape(1, d))


def _alibi_slopes(n):
    return 2.0 ** (-8.0 * np.arange(1, n + 1) / n)


def kernel(x, c, ada_w, ada_b, attn_norm_g, ffn_norm_g, w_in_even, w_out_even, lam_q1, lam_k1,
           lam_q2, lam_k2, subln_g, w_in_odd, b_forget, w_out_odd, router_w, router_b, moe_w1,
           moe_w3, moe_w2, final_norm_g):
    bsz, seq, d = x.shape
    depth = ada_w.shape[0]
    n_tok = bsz * seq
    x2 = x.reshape(n_tok, d)

    mod_all = _adaln(c, ada_w, ada_b)

    slopes = _alibi_slopes(A_HEADS + B_HEADS)
    slopes_a = jnp.asarray(slopes[0::2] * LOG2E, F32)
    slopes_b = jnp.asarray(slopes[1::2] * LOG2E, F32)
    a_width = A_HEADS * HEAD_DIM
    b_width = B_HEADS * HEAD_DIM
    c_width = C_HEADS * HEAD_DIM
    cs_even = np.ones((1, 3 * a_width + 3 * b_width), np.float32)
    cs_even[0, :a_width] = A_QK_DIM ** -0.5 * LOG2E
    cs_even[0, 3 * a_width:3 * a_width + b_width] = HEAD_DIM ** -0.5 * LOG2E
    cs_odd = np.ones((1, 3 * c_width), np.float32)
    cs_odd[0, :c_width] = HEAD_DIM ** -0.5 * LOG2E

    router_w_pad = jnp.pad(router_w, ((0, 0), (0, LANES - N_EXPERTS)))
    router_b_t = router_b.reshape(N_EXPERTS, 1)

    for l in range(depth):
        mod = mod_all[l]
        j = l // 2
        if l % 2 == 0:
            (qkv,) = _inproj(x2, attn_norm_g[l], mod, w_in_even[j].astype(BF16),
                             jnp.asarray(cs_even), seq=seq)
            qkv = qkv.reshape(bsz, seq, -1)
            lam_init = 0.8 - 0.6 * math.exp(-0.3 * l)
            lam_rows = jnp.zeros((SUBLANES, LANES), F32)
            lam_rows = lam_rows.at[0:4, 0:A_QK_DIM].set(
                jnp.stack([lam_q1[j], lam_k1[j], lam_q2[j], lam_k2[j]]).astype(F32))
            a_out = _attention(qkv, mode="diff", n_heads=A_HEADS, q_col=0, k_col=A_HEADS,
                               v_col=2 * A_HEADS, slopes=slopes_a,
                               lam_rows=lam_rows, subln_g=subln_g[j].reshape(1, HEAD_DIM),
                               lam_init=lam_init)
            b_out = _attention(qkv, mode="moba", n_heads=B_HEADS, q_col=3 * A_HEADS,
                               k_col=3 * A_HEADS + B_HEADS, v_col=3 * A_HEADS + 2 * B_HEADS,
                               slopes=slopes_b)
            parts = [(a_out.reshape(n_tok, a_width), 0), (b_out.reshape(n_tok, b_width), 0)]
            w_out = w_out_even[j]
        else:
            w_in = w_in_odd[j]
            wf = jnp.pad(w_in[:, 3 * c_width:], ((0, 0), (0, LANES - C_HEADS)))
            qkv, z = _inproj(x2, attn_norm_g[l], mod, w_in.astype(BF16), jnp.asarray(cs_odd), wf,
                             seq=seq)
            qkv = qkv.reshape(bsz, seq, -1)
            bf_row = jnp.pad(b_forget[j].astype(F32), (0, LANES - C_HEADS)).reshape(1, LANES)
            cum = _forget_cumsum(z.reshape(bsz, seq, LANES), bf_row)
            tk = ATTN_TILES["fox"][1]
            cum = cum[:, :C_HEADS, :].reshape(bsz, C_HEADS, seq // tk, tk)
            c_out = _attention(qkv, mode="fox", n_heads=C_HEADS, q_col=0, k_col=C_HEADS,
                               v_col=2 * C_HEADS, cum=cum)
            c_out = c_out.reshape(n_tok, c_width)
            parts = [(c_out, 0), (c_out, 1)]
            w_out = w_out_odd[j]
        x2, h2, gates, group_rows = _outproj(parts, w_out.astype(BF16), x2, mod, ffn_norm_g[l],
                                             router_w_pad, router_b_t, seq=seq)
        y = _moe(h2, gates, group_rows, moe_w1[l].astype(BF16), moe_w3[l].astype(BF16),
                 moe_w2[l].astype(BF16))
        x2 = _combine(x2, y, mod, final_norm_g, seq=seq, final_norm=(l == depth - 1))
    return x2.reshape(bsz, seq, d)
```

```python
import functools
import math

import jax
import jax.numpy as jnp
import numpy as np
from jax import lax
from jax.experimental import pallas as pl
from jax.experimental.pallas import tpu as pltpu

F32 = jnp.float32
BF16 = jnp.bfloat16
HIGHEST = lax.Precision.HIGHEST

HEAD_DIM = 128
A_HEADS = 8
A_QK_DIM = 64
B_HEADS = 8
C_HEADS = 16
MOBA_BLOCK = 256
MOBA_TOPK = 3
N_EXPERTS = 16
N_GROUPS = 4
EXPERTS_PER_GROUP = N_EXPERTS // N_GROUPS
EPS = 1e-6
NEG_INF = -1e30
LOG2E = 1.4426950408889634
GROUP_LANE = N_EXPERTS

LANES = 128
SUBLANES = 8
VMEM_LIMIT_BYTES = 56 * 1024 * 1024

NT_DIMS = (((1,), (1,)), ((), ()))


def _dot_bf16x3(a, b):
    a_hi = a.astype(BF16)
    a_lo = (a - a_hi.astype(F32)).astype(BF16)
    b_hi = b.astype(BF16)
    b_lo = (b - b_hi.astype(F32)).astype(BF16)
    return (jnp.dot(a_hi, b_hi, preferred_element_type=F32)
            + jnp.dot(a_hi, b_lo, preferred_element_type=F32)
            + jnp.dot(a_lo, b_hi, preferred_element_type=F32))


def _lane_repeat(x, n):
    return x if n == 1 else jnp.concatenate([x] * n, axis=1)


def _compiler_params(semantics):
    return pltpu.CompilerParams(dimension_semantics=semantics, vmem_limit_bytes=VMEM_LIMIT_BYTES)


def _adaln_kernel(c_ref, w_ref, b_ref, o_ref):
    c = c_ref[...]
    cond = c * jax.nn.sigmoid(c)
    o_ref[0] = jnp.dot(cond, w_ref[0], precision=HIGHEST, preferred_element_type=F32) + b_ref[0]


def _adaln(c, ada_w, ada_b, *, tn=1024):
    depth, d, n = ada_w.shape
    bsz = c.shape[0]
    rows = -(-bsz // SUBLANES) * SUBLANES
    c_pad = jnp.pad(c, ((0, rows - bsz), (0, 0)))
    out = pl.pallas_call(
        _adaln_kernel,
        out_shape=jax.ShapeDtypeStruct((depth, rows, n), F32),
        grid=(depth, n // tn),
        in_specs=[
            pl.BlockSpec((rows, d), lambda l, j: (0, 0)),
            pl.BlockSpec((1, d, tn), lambda l, j: (l, 0, j)),
            pl.BlockSpec((1, 1, tn), lambda l, j: (l, 0, j)),
        ],
        out_specs=pl.BlockSpec((1, rows, tn), lambda l, j: (l, 0, j)),
        compiler_params=_compiler_params(("arbitrary", "arbitrary")),
        name="adaln",
    )(c_pad, ada_w, ada_b.reshape(depth, 1, n))
    return out[:, :bsz].reshape(depth, bsz, 6, d)


def _modulated_norm(x, gain_row, mod_ref, shift_idx, scale_idx):
    ms = jnp.mean(x * x, axis=-1, keepdims=True)
    xn = x * lax.rsqrt(ms + EPS)
    gain = gain_row * (1.0 + mod_ref[0, scale_idx:scale_idx + 1, :])
    return xn * gain + mod_ref[0, shift_idx:shift_idx + 1, :]


def _inproj_kernel(*refs, with_forget):
    if with_forget:
        x_ref, g_ref, mod_ref, w_ref, cs_ref, wf_ref, o_ref, z_ref, h_sc = refs
    else:
        x_ref, g_ref, mod_ref, w_ref, cs_ref, o_ref, h_sc = refs

    @pl.when(pl.program_id(1) == 0)
    def _():
        h = _modulated_norm(x_ref[...], g_ref[...], mod_ref, 0, 1)
        h_sc[...] = h.astype(BF16)
        if with_forget:
            z_ref[...] = _dot_bf16x3(h, wf_ref[...])

    acc = jnp.dot(h_sc[...], w_ref[...], preferred_element_type=F32)
    o_ref[...] = (acc * cs_ref[...]).astype(o_ref.dtype)


def _inproj(x2, gain, mod, w_bf16, col_scale, wf=None, *, seq, tm=1024, tn=1024):
    n_tok, d = x2.shape
    n_out = col_scale.shape[1]
    tiles_per_seq = seq // tm
    with_forget = wf is not None
    in_specs = [
        pl.BlockSpec((tm, d), lambda i, j: (i, 0)),
        pl.BlockSpec((1, d), lambda i, j: (0, 0)),
        pl.BlockSpec((1, 6, d), lambda i, j: (i // tiles_per_seq, 0, 0)),
        pl.BlockSpec((d, tn), lambda i, j: (0, j)),
        pl.BlockSpec((1, tn), lambda i, j: (0, j)),
    ]
    args = [x2, gain.reshape(1, d), mod, w_bf16, col_scale]
    out_shape = [jax.ShapeDtypeStruct((n_tok, n_out), BF16)]
    out_specs = [pl.BlockSpec((tm, tn), lambda i, j: (i, j))]
    if with_forget:
        in_specs.append(pl.BlockSpec((d, LANES), lambda i, j: (0, 0)))
        args.append(wf)
        out_shape.append(jax.ShapeDtypeStruct((n_tok, LANES), F32))
        out_specs.append(pl.BlockSpec((tm, LANES), lambda i, j: (i, 0)))
    return pl.pallas_call(
        functools.partial(_inproj_kernel, with_forget=with_forget),
        out_shape=out_shape,
        grid=(n_tok // tm, n_out // tn),
        in_specs=in_specs,
        out_specs=out_specs,
        scratch_shapes=[pltpu.VMEM((tm, d), BF16)],
        compiler_params=_compiler_params(("arbitrary", "arbitrary")),
        name="inproj_forget" if with_forget else "inproj",
    )(*args)


def _fcum_kernel(z_ref, bf_ref, o_ref, carry_sc):
    @pl.when(pl.program_id(1) == 0)
    def _():
        carry_sc[...] = jnp.zeros_like(carry_sc)

    z = z_ref[0] + bf_ref[...]
    log_f = jnp.minimum(z, 0.0) - jnp.log1p(jnp.exp(-jnp.abs(z)))
    tc = z.shape[0]
    row = lax.broadcasted_iota(jnp.int32, (tc, tc), 0)
    col = lax.broadcasted_iota(jnp.int32, (tc, tc), 1)
    tri = jnp.where(row >= col, 1.0, 0.0).astype(F32)
    cum = jnp.dot(tri, log_f, precision=HIGHEST, preferred_element_type=F32) + carry_sc[...]
    carry_sc[...] = cum[tc - 1:tc, :]
    o_ref[0] = cum.T


def _forget_cumsum(z, b_forget_row, *, tc=256):
    bsz, seq, _ = z.shape
    return pl.pallas_call(
        _fcum_kernel,
        out_shape=jax.ShapeDtypeStruct((bsz, LANES, seq), F32),
        grid=(bsz, seq // tc),
        in_specs=[
            pl.BlockSpec((1, tc, LANES), lambda b, t: (b, t, 0)),
            pl.BlockSpec((1, LANES), lambda b, t: (0, 0)),
        ],
        out_specs=pl.BlockSpec((1, LANES, tc), lambda b, t: (b, 0, t)),
        scratch_shapes=[pltpu.VMEM((1, LANES), F32)],
        compiler_params=_compiler_params(("arbitrary", "arbitrary")),
        name="forget_cumsum",
    )(z, b_forget_row)


def _attn_kernel(*refs, mode, tq, tk, seq, lam_init):
    if mode == "diff":
        slopes_ref, q_ref, k_ref, v_ref, lam_ref, subg_ref, o_ref = refs[:7]
        m_sc, l_sc, acc_sc, sa_sc, sb_sc, vx_sc = refs[7:]
    elif mode == "moba":
        slopes_ref, q_ref, k_ref, v_ref, o_ref = refs[:5]
        m_sc, l_sc, acc_sc, sa_sc, sb_sc, vx_sc, kmean_sc, sel_sc, kx_sc = refs[5:]
    else:
        q_ref, k_ref, v_ref, cum_ref, o_ref = refs[:5]
        m_sc, l_sc, acc_sc, sa_sc, sb_sc, vx_sc = refs[5:]

    head = pl.program_id(1)
    qi = pl.program_id(2)
    q0 = qi * tq
    rows = 2 * tq if mode == "diff" else tq
    reps = tk // LANES
    n_diag = tq // tk

    q = q_ref[0]
    if mode == "diff":
        lane = lax.broadcasted_iota(jnp.int32, (tq, HEAD_DIM), 1)
        zero = jnp.zeros_like(q)
        q = jnp.concatenate([jnp.where(lane < A_QK_DIM, q, zero),
                             jnp.where(lane >= A_QK_DIM, q, zero)], axis=0)

    def col_bias(j):
        if mode == "fox":
            return cum_ref[0, 0, pl.ds(j, 1), :] * (-LOG2E)
        kpos = lax.broadcasted_iota(jnp.int32, (1, tk), 1) + (j * tk - q0)
        return kpos.astype(F32) * slopes_ref[head]

    def scores(j, r0=0):
        start = pl.multiple_of(j * tk, tk)
        k = k_ref[0, pl.ds(start, tk), :]
        s = lax.dot_general(q[r0:], k, NT_DIMS, preferred_element_type=F32)
        return s + col_bias(j), start

    def chosen_bias(j, r0=0):
        jf = j.astype(F32)
        return jnp.where(sel_sc[0, r0:, :] == jf, 0.0,
                         jnp.where(sel_sc[1, r0:, :] == jf, 0.0,
                                   jnp.where(sel_sc[2, r0:, :] == jf, 0.0, NEG_INF)))

    if mode == "moba":
        n_blocks = seq // MOBA_BLOCK
        sub_per_tile = tq // MOBA_BLOCK

        @pl.when(qi == 0)
        def _():
            kmean_sc[...] = jnp.zeros_like(kmean_sc)
            for n in range(n_blocks):
                blk = k_ref[0, n * MOBA_BLOCK:(n + 1) * MOBA_BLOCK, :].astype(F32)
                kmean_sc[n:n + 1, :] = jnp.mean(blk, axis=0, keepdims=True)

        gate = sum(lax.dot_general(q, part, NT_DIMS, preferred_element_type=F32)
                   for part in _split_bf16x3(kmean_sc[...]))
        blk_id = lax.broadcasted_iota(jnp.int32, (tq, LANES), 1).astype(F32)
        row_sub = lax.broadcasted_iota(jnp.int32, (tq, LANES), 0) // MOBA_BLOCK
        own = (row_sub + qi * sub_per_tile).astype(F32)
        gate = jnp.where(blk_id < own, gate, NEG_INF)
        block_bias = jnp.full((tq, LANES), NEG_INF, F32)
        for r in range(MOBA_TOPK):
            best = jnp.max(gate, axis=1, keepdims=True)
            pick = jnp.min(jnp.where(gate == best, blk_id, float(LANES)), axis=1, keepdims=True)
            pick = jnp.where(pick < own, pick, -1.0)
            sel_sc[r] = jnp.broadcast_to(pick, (tq, LANES))
            block_bias = jnp.where(blk_id == pick, 0.0, block_bias)
            gate = jnp.where(blk_id == pick, -jnp.inf, gate)

        slope = slopes_ref[head]

        @pl.when(qi == 0)
        def _():
            kx_sc[:, 0:HEAD_DIM] = k_ref[0]
            lane_k = lax.broadcasted_iota(jnp.int32, (seq, LANES), 1)
            kpos = lax.broadcasted_iota(jnp.int32, (seq, LANES), 0)
            hi, mid, lo = _split_bf16x3(kpos.astype(F32) * slope)
            aux = jnp.where(lane_k == kpos // MOBA_BLOCK, 1.0, 0.0)
            aux = jnp.where(lane_k == n_blocks, hi.astype(F32), aux)
            aux = jnp.where(lane_k == n_blocks + 1, mid.astype(F32), aux)
            aux = jnp.where(lane_k == n_blocks + 2, lo.astype(F32), aux)
            aux = jnp.where((lane_k >= n_blocks + 3) & (lane_k < n_blocks + 6), 1.0, aux)
            kx_sc[:, HEAD_DIM:2 * HEAD_DIM] = aux.astype(BF16)

        lane_q = lax.broadcasted_iota(jnp.int32, (tq, LANES), 1)
        off = jnp.full((tq, LANES), 0.0, F32) - slope * q0.astype(F32)
        hi, mid, lo = _split_bf16x3(off)
        aux_q = jnp.where(lane_q < n_blocks, block_bias, 0.0)
        aux_q = jnp.where((lane_q >= n_blocks) & (lane_q < n_blocks + 3), 1.0, aux_q)
        aux_q = jnp.where(lane_q == n_blocks + 3, hi.astype(F32), aux_q)
        aux_q = jnp.where(lane_q == n_blocks + 4, mid.astype(F32), aux_q)
        aux_q = jnp.where(lane_q == n_blocks + 5, lo.astype(F32), aux_q)
        q_wide = jnp.concatenate([q, aux_q.astype(BF16)], axis=1)

    def past_scores(j):
        if mode != "moba":
            return scores(j)[0]
        start = pl.multiple_of(j * tk, tk)
        return lax.dot_general(q_wide, kx_sc[pl.ds(start, tk), :], NT_DIMS,
                               preferred_element_type=F32)

    @pl.when(qi == 0)
    def _():
        vx_sc[:, 0:HEAD_DIM] = v_ref[0]
        vx_sc[:, HEAD_DIM:2 * HEAD_DIM] = jnp.ones((seq, LANES), BF16)

    def accumulate(s, start, first, r0=0):
        v = vx_sc[pl.ds(start, tk), :]
        if first:
            m_new = jnp.max(s, axis=1, keepdims=True)
            p = jnp.exp2(s - m_new)
            pv = jnp.dot(p.astype(BF16), v, preferred_element_type=F32)
            m_sc[...] = jnp.broadcast_to(m_new, (rows, LANES))
            l_sc[...] = pv[:, HEAD_DIM:2 * HEAD_DIM]
            acc_sc[...] = pv[:, 0:HEAD_DIM]
            return
        m_prev = m_sc[r0:, :]
        m_new = jnp.maximum(m_prev, jnp.max(s, axis=1, keepdims=True))
        p = jnp.exp2(s - _lane_repeat(m_new, reps))
        alpha = jnp.exp2(m_prev - m_new)
        pv = jnp.dot(p.astype(BF16), v, preferred_element_type=F32)
        l_sc[r0:, :] = alpha * l_sc[r0:, :] + pv[:, HEAD_DIM:2 * HEAD_DIM]
        acc_sc[r0:, :] = alpha * acc_sc[r0:, :] + pv[:, 0:HEAD_DIM]
        m_sc[r0:, :] = m_new

    jd = qi * n_diag
    for c in range(n_diag):
        r0 = 0 if mode == "diff" else c * tk
        s, start = scores(jd + c, r0)
        rpos = lax.broadcasted_iota(jnp.int32, (rows - r0, tk), 0) + r0
        if mode == "diff":
            rpos = jnp.where(rpos >= tq, rpos - tq, rpos)
        cpos = lax.broadcasted_iota(jnp.int32, (rows - r0, tk), 1) + c * tk
        causal = cpos <= rpos
        if mode == "moba":
            s_sel = s + _lane_repeat(chosen_bias(jd + c, r0), reps)
            s = jnp.where(rpos < (c + 1) * MOBA_BLOCK, jnp.where(causal, s, NEG_INF), s_sel)
        else:
            s = jnp.where(causal, s, NEG_INF)
        accumulate(s, start, first=(c == 0), r0=r0)

    def consume(s, j):
        accumulate(s, pl.multiple_of(j * tk, tk), first=False)

    def past_pair(jj, carry):
        j0 = 2 * jj
        sb_sc[...] = past_scores(j0 + 1)
        consume(sa_sc[...], j0)
        sa_sc[...] = past_scores(jnp.minimum(j0 + 2, jd - 2))
        consume(sb_sc[...], j0 + 1)
        return carry

    sa_sc[...] = past_scores(0)
    lax.fori_loop(0, qi * (n_diag // 2), past_pair, 0)

    out = acc_sc[...] / l_sc[...]
    if mode == "diff":
        lam_rows = lam_ref[...]
        lam = (jnp.exp(jnp.sum(lam_rows[0:1] * lam_rows[1:2], axis=1, keepdims=True))
               - jnp.exp(jnp.sum(lam_rows[2:3] * lam_rows[3:4], axis=1, keepdims=True)) + lam_init)
        a = out[:tq] - lam * out[tq:]
        ms = jnp.mean(a * a, axis=-1, keepdims=True)
        out = a * lax.rsqrt(ms + EPS) * subg_ref[...] * (1.0 - lam_init)
    o_ref[0] = out.astype(o_ref.dtype)


ATTN_TILES = {"diff": (1024, 512), "moba": (1024, MOBA_BLOCK), "fox": (1024, 512)}


def _attention(qkv, *, mode, n_heads, q_col, k_col, v_col, slopes=None, lam_rows=None,
               subln_g=None, cum=None, lam_init=0.0):
    bsz, seq, _ = qkv.shape
    tq, tk = ATTN_TILES[mode]
    tq = min(tq, seq)
    assert tq % (2 * tk) == 0 and seq % tq == 0
    rows = 2 * tq if mode == "diff" else tq
    q_spec = pl.BlockSpec((1, tq, HEAD_DIM), lambda b, h, i: (b, i, q_col + h))
    k_spec = pl.BlockSpec((1, seq, HEAD_DIM), lambda b, h, i: (b, 0, k_col + h))
    v_spec = pl.BlockSpec((1, seq, HEAD_DIM), lambda b, h, i: (b, 0, v_col + h))
    smem = pl.BlockSpec(memory_space=pltpu.SMEM)
    scratch = [pltpu.VMEM((rows, LANES), F32), pltpu.VMEM((rows, LANES), F32),
               pltpu.VMEM((rows, HEAD_DIM), F32),
               pltpu.VMEM((rows, tk), F32), pltpu.VMEM((rows, tk), F32),
               pltpu.VMEM((seq, 2 * HEAD_DIM), BF16)]
    if mode == "diff":
        in_specs = [smem, q_spec, k_spec, v_spec,
                    pl.BlockSpec((SUBLANES, LANES), lambda b, h, i: (0, 0)),
                    pl.BlockSpec((1, HEAD_DIM), lambda b, h, i: (0, 0))]
        args = [slopes, qkv, qkv, qkv, lam_rows, subln_g]
    elif mode == "moba":
        assert tk == MOBA_BLOCK and seq // MOBA_BLOCK <= LANES
        in_specs = [smem, q_spec, k_spec, v_spec]
        args = [slopes, qkv, qkv, qkv]
        scratch += [pltpu.VMEM((LANES, HEAD_DIM), F32), pltpu.VMEM((MOBA_TOPK, tq, LANES), F32),
                    pltpu.VMEM((seq, 2 * HEAD_DIM), BF16)]
    else:
        in_specs = [q_spec, k_spec, v_spec,
                    pl.BlockSpec((1, 1, seq // tk, tk), lambda b, h, i: (b, h, 0, 0))]
        args = [qkv, qkv, qkv, cum]
    return pl.pallas_call(
        functools.partial(_attn_kernel, mode=mode, tq=tq, tk=tk, seq=seq, lam_init=lam_init),
        out_shape=jax.ShapeDtypeStruct((bsz, seq, n_heads * HEAD_DIM), BF16),
        grid=(bsz, n_heads, seq // tq),
        in_specs=in_specs,
        out_specs=pl.BlockSpec((1, tq, HEAD_DIM), lambda b, h, i: (b, i, h)),
        scratch_shapes=scratch,
        compiler_params=_compiler_params(("arbitrary", "arbitrary", "arbitrary")),
        name="attn_" + mode,
    )(*args)


def _route(logits_t, bias_t):
    aff = jax.nn.sigmoid(logits_t)
    sel = aff + bias_t
    sel_rows = [sel[e:e + 1, :] for e in range(N_EXPERTS)]
    aff_rows = [aff[e:e + 1, :] for e in range(N_EXPERTS)]

    def first_argmax(vals):
        idx = jnp.zeros_like(vals[0])
        top = vals[0]
        for r in range(1, len(vals)):
            better = vals[r] > top
            idx = jnp.where(better, float(r), idx)
            top = jnp.where(better, vals[r], top)
        return idx

    group_scores = []
    for g in range(N_GROUPS):
        a, b, c, d = sel_rows[EXPERTS_PER_GROUP * g:EXPERTS_PER_GROUP * (g + 1)]
        group_scores.append(jnp.maximum(jnp.maximum(jnp.maximum(a + b, a + c), jnp.maximum(a + d, b + c)),
                                        jnp.maximum(b + d, c + d)))
    best = first_argmax(group_scores)

    def in_best_group(rows_, r):
        out = rows_[(N_GROUPS - 1) * EXPERTS_PER_GROUP + r]
        for g in range(N_GROUPS - 2, -1, -1):
            out = jnp.where(best == float(g), rows_[g * EXPERTS_PER_GROUP + r], out)
        return out

    cand_sel = [in_best_group(sel_rows, r) for r in range(EXPERTS_PER_GROUP)]
    cand_aff = [in_best_group(aff_rows, r) for r in range(EXPERTS_PER_GROUP)]
    i1 = first_argmax(cand_sel)
    i2 = first_argmax([jnp.where(i1 == float(r), -jnp.inf, cand_sel[r])
                       for r in range(EXPERTS_PER_GROUP)])

    def pick(idx):
        out = cand_aff[EXPERTS_PER_GROUP - 1]
        for r in range(EXPERTS_PER_GROUP - 2, -1, -1):
            out = jnp.where(idx == float(r), cand_aff[r], out)
        return out

    w1, w2 = pick(i1), pick(i2)
    den = w1 + w2
    w1, w2 = w1 / den, w2 / den
    gate_rows = []
    for e in range(N_EXPERTS):
        g, r = divmod(e, EXPERTS_PER_GROUP)
        in_expert = jnp.where(i1 == float(r), w1, jnp.where(i2 == float(r), w2, 0.0))
        gate_rows.append(jnp.where(best == float(g), in_expert, 0.0))
    return jnp.concatenate(gate_rows, axis=0), best


def _outproj_kernel(a_ref, b_ref, w_ref, x_ref, mod_ref, g_ref, rw_ref, rb_ref,
                    xo_ref, h_ref, gates_ref, group_ref):
    half = a_ref.shape[1]
    y = (jnp.dot(a_ref[...], w_ref[0:half, :], preferred_element_type=F32)
         + jnp.dot(b_ref[...], w_ref[half:2 * half, :], preferred_element_type=F32))
    x_new = x_ref[...] + mod_ref[0, 2:3, :] * y
    xo_ref[...] = x_new
    h = _modulated_norm(x_new, g_ref[...], mod_ref, 3, 4)
    h_ref[...] = h.astype(BF16)
    logits = _dot_bf16x3(h, rw_ref[...])
    gates_t, best = _route(logits.T[0:N_EXPERTS, :], rb_ref[...])
    tm = gates_t.shape[1]
    padded = jnp.concatenate(
        [gates_t, best, jnp.zeros((LANES - N_EXPERTS - 1, tm), F32)], axis=0)
    gates_ref[...] = padded.T
    group_ref[0] = best


def _outproj(parts, w_bf16, x2, mod, ffn_gain, router_w_pad, router_b_t, *, seq, tm=512):
    n_tok, d = x2.shape
    half = d // 2
    tiles_per_seq = seq // tm
    (a, ca), (b, cb) = parts
    return pl.pallas_call(
        _outproj_kernel,
        out_shape=[jax.ShapeDtypeStruct((n_tok, d), F32),
                   jax.ShapeDtypeStruct((n_tok, d), BF16),
                   jax.ShapeDtypeStruct((n_tok, LANES), F32),
                   jax.ShapeDtypeStruct((n_tok // tm, 1, tm), F32)],
        grid=(n_tok // tm,),
        in_specs=[
            pl.BlockSpec((tm, half), lambda i: (i, ca)),
            pl.BlockSpec((tm, half), lambda i: (i, cb)),
            pl.BlockSpec((d, d), lambda i: (0, 0)),
            pl.BlockSpec((tm, d), lambda i: (i, 0)),
            pl.BlockSpec((1, 6, d), lambda i: (i // tiles_per_seq, 0, 0)),
            pl.BlockSpec((1, d), lambda i: (0, 0)),
            pl.BlockSpec((d, LANES), lambda i: (0, 0)),
            pl.BlockSpec((N_EXPERTS, tm), lambda i: (0, 0)),
        ],
        out_specs=[pl.BlockSpec((tm, d), lambda i: (i, 0)),
                   pl.BlockSpec((tm, d), lambda i: (i, 0)),
                   pl.BlockSpec((tm, LANES), lambda i: (i, 0)),
                   pl.BlockSpec((1, 1, tm), lambda i: (i, 0, 0))],
        compiler_params=_compiler_params(("arbitrary",)),
        name="outproj_router",
    )(a, b, w_bf16, x2, mod, ffn_gain.reshape(1, d), router_w_pad,
      jnp.broadcast_to(router_b_t, (N_EXPERTS, tm)))


MOE_CHUNK = 256
MOE_TAIL_CHUNK = 128


def _split_bf16x3(x):
    hi = x.astype(BF16)
    r1 = x - hi.astype(F32)
    mid = r1.astype(BF16)
    lo = (r1 - mid.astype(F32)).astype(BF16)
    return hi, mid, lo


def _moe_kernel(h_ref, gates_ref, group_ref, w1_ref, w3_ref, w2_ref, o_ref, acc_sc):
    g = pl.program_id(0)
    gf = g.astype(F32)
    tm = h_ref.shape[0]
    n_local = w1_ref.shape[0]

    gates = gates_ref[...]
    lane = lax.broadcasted_iota(jnp.int32, (tm, LANES), 1)
    grp_col = jnp.sum(jnp.where(lane == GROUP_LANE, gates, 0.0), axis=1, keepdims=True)
    member_col = jnp.broadcast_to(jnp.where(grp_col == gf, 1.0, 0.0), (tm, LANES))
    member_row = jnp.broadcast_to(jnp.where(group_ref[0] == gf, 1.0, 0.0), (SUBLANES, tm))

    ri = lax.broadcasted_iota(jnp.int32, (tm, tm), 0)
    ci = lax.broadcasted_iota(jnp.int32, (tm, tm), 1)
    lower = jnp.where(ci < ri, 1.0, 0.0).astype(BF16)
    upper = jnp.where(ri < ci, 1.0, 0.0).astype(BF16)
    rank_col = jnp.dot(lower, member_col.astype(BF16), preferred_element_type=F32)
    rank_row = jnp.dot(member_row.astype(BF16), upper, preferred_element_type=F32)[0:1, :]
    pos_col = jnp.where(member_col > 0.0, rank_col, -1.0)
    pos_row = jnp.where(member_row[0:1, :] > 0.0, rank_row, -1.0)
    count = jnp.sum(member_row[0:1, :]).astype(jnp.int32)
    n_tail = lax.shift_right_logical(jnp.maximum(count - MOE_CHUNK, 0) + (MOE_TAIL_CHUNK - 1),
                                     MOE_TAIL_CHUNK.bit_length() - 1)

    gates3 = jnp.concatenate(_split_bf16x3(gates), axis=1)
    acc_sc[...] = jnp.zeros_like(acc_sc)

    def run_chunk(size, base):
        slot_r = lax.broadcasted_iota(jnp.int32, (size, tm), 0).astype(F32) + base
        slot_c = lax.broadcasted_iota(jnp.int32, (tm, size), 1).astype(F32) + base
        gather = jnp.where(pos_row == slot_r, 1.0, 0.0).astype(BF16)
        scatter = jnp.where(_lane_repeat(pos_col, size // LANES) == slot_c,
                            1.0, 0.0).astype(BF16)
        hc = jnp.dot(gather, h_ref[...], preferred_element_type=F32).astype(BF16)
        g3 = jnp.dot(gather, gates3, preferred_element_type=F32)
        gc = g3[:, 0:LANES] + g3[:, LANES:2 * LANES] + g3[:, 2 * LANES:3 * LANES]
        clane = lax.broadcasted_iota(jnp.int32, (size, LANES), 1)
        y = None
        for el in range(n_local):
            gate_col = jnp.sum(jnp.where(clane == g * n_local + el, gc, 0.0), axis=1, keepdims=True)
            u = jnp.dot(hc, w1_ref[el], preferred_element_type=F32)
            v = jnp.dot(hc, w3_ref[el], preferred_element_type=F32)
            act = ((u * jax.nn.sigmoid(u)) * v * gate_col).astype(BF16)
            part = jnp.dot(act, w2_ref[el], preferred_element_type=F32)
            y = part if y is None else y + part
        acc_sc[...] += jnp.dot(scatter, y.astype(BF16), preferred_element_type=F32)

    @pl.when(count > 0)
    def _():
        run_chunk(MOE_CHUNK, 0.0)

    def tail_chunk(k, carry):
        run_chunk(MOE_TAIL_CHUNK, (MOE_CHUNK + k * MOE_TAIL_CHUNK).astype(F32))
        return carry

    lax.fori_loop(0, n_tail, tail_chunk, 0)
    o_ref[0] = acc_sc[...].astype(o_ref.dtype)


def _moe(h2, gates, group_rows, w1, w3, w2, *, tm=1024):
    n_tok, d = h2.shape
    n_exp, _, d_ff = w1.shape
    tm = min(tm, n_tok)
    per_group = n_exp // N_GROUPS
    once = pl.Buffered(1)
    return pl.pallas_call(
        _moe_kernel,
        out_shape=jax.ShapeDtypeStruct((N_GROUPS, n_tok, d), BF16),
        grid=(N_GROUPS, n_tok // tm),
        in_specs=[
            pl.BlockSpec((tm, d), lambda g, i: (i, 0)),
            pl.BlockSpec((tm, LANES), lambda g, i: (i, 0)),
            pl.BlockSpec((1, 1, tm), lambda g, i: (i, 0, 0)),
            pl.BlockSpec((per_group, d, d_ff), lambda g, i: (g, 0, 0), pipeline_mode=once),
            pl.BlockSpec((per_group, d, d_ff), lambda g, i: (g, 0, 0), pipeline_mode=once),
            pl.BlockSpec((per_group, d_ff, d), lambda g, i: (g, 0, 0), pipeline_mode=once),
        ],
        out_specs=pl.BlockSpec((1, tm, d), lambda g, i: (g, i, 0)),
        scratch_shapes=[pltpu.VMEM((tm, d), F32)],
        compiler_params=_compiler_params(("arbitrary", "arbitrary")),
        name="moe_grouped",
    )(h2, gates, group_rows.reshape(n_tok // tm, 1, tm), w1, w3, w2)


def _combine_kernel(x_ref, y_ref, mod_ref, fg_ref, o_ref, *, final_norm):
    y = y_ref[0].astype(F32)
    for g in range(1, y_ref.shape[0]):
        y = y + y_ref[g].astype(F32)
    x_new = x_ref[...] + mod_ref[0, 5:6, :] * y
    if final_norm:
        ms = jnp.mean(x_new * x_new, axis=-1, keepdims=True)
        x_new = x_new * lax.rsqrt(ms + EPS) * fg_ref[...]
    o_ref[...] = x_new


def _combine(x2, y, mod, final_gain, *, seq, final_norm, tm=512):
    n_tok, d = x2.shape
    tm = min(tm, seq)
    tiles_per_seq = seq // tm
    return pl.pallas_call(
        functools.partial(_combine_kernel, final_norm=final_norm),
        out_shape=jax.ShapeDtypeStruct((n_tok, d), F32),
        grid=(n_tok // tm,),
        in_specs=[
            pl.BlockSpec((tm, d), lambda i: (i, 0)),
            pl.BlockSpec((y.shape[0], tm, d), lambda i: (0, i, 0)),
            pl.BlockSpec((1, 6, d), lambda i: (i // tiles_per_seq, 0, 0)),
            pl.BlockSpec((1, d), lambda i: (0, 0)),
        ],
        out_specs=pl.BlockSpec((tm, d), lambda i: (i, 0)),
        compiler_params=_compiler_params(("arbitrary",)),
        name="combine_final" if final_norm else "combine",
    )(x2, y, mod, final_gain.reshape(1, d))


def _alibi_slopes(n):
    return 2.0 ** (-8.0 * np.arange(1, n + 1) / n)


def kernel(x, c, ada_w, ada_b, attn_norm_g, ffn_norm_g, w_in_even, w_out_even, lam_q1, lam_k1,
           lam_q2, lam_k2, subln_g, w_in_odd, b_forget, w_out_odd, router_w, router_b, moe_w1,
           moe_w3, moe_w2, final_norm_g):
    bsz, seq, d = x.shape
    depth = ada_w.shape[0]
    n_tok = bsz * seq
    x2 = x.reshape(n_tok, d)

    mod_all = _adaln(c, ada_w, ada_b)

    slopes = _alibi_slopes(A_HEADS + B_HEADS)
    slopes_a = jnp.asarray(slopes[0::2] * LOG2E, F32)
    slopes_b = jnp.asarray(slopes[1::2] * LOG2E, F32)
    a_width = A_HEADS * HEAD_DIM
    b_width = B_HEADS * HEAD_DIM
    c_width = C_HEADS * HEAD_DIM
    cs_even = np.ones((1, 3 * a_width + 3 * b_width), np.float32)
    cs_even[0, :a_width] = A_QK_DIM ** -0.5 * LOG2E
    cs_even[0, 3 * a_width:3 * a_width + b_width] = HEAD_DIM ** -0.5 * LOG2E
    cs_odd = np.ones((1, 3 * c_width), np.float32)
    cs_odd[0, :c_width] = HEAD_DIM ** -0.5 * LOG2E

    router_w_pad = jnp.pad(router_w, ((0, 0), (0, LANES - N_EXPERTS)))
    router_b_t = router_b.reshape(N_EXPERTS, 1)

    for l in range(depth):
        mod = mod_all[l]
        j = l // 2
        if l % 2 == 0:
            (qkv,) = _inproj(x2, attn_norm_g[l], mod, w_in_even[j].astype(BF16),
                             jnp.asarray(cs_even), seq=seq)
            qkv = qkv.reshape(bsz, seq, -1)
            lam_init = 0.8 - 0.6 * math.exp(-0.3 * l)
            lam_rows = jnp.zeros((SUBLANES, LANES), F32)
            lam_rows = lam_rows.at[0:4, 0:A_QK_DIM].set(
                jnp.stack([lam_q1[j], lam_k1[j], lam_q2[j], lam_k2[j]]).astype(F32))
            a_out = _attention(qkv, mode="diff", n_heads=A_HEADS, q_col=0, k_col=A_HEADS,
                               v_col=2 * A_HEADS, slopes=slopes_a,
                               lam_rows=lam_rows, subln_g=subln_g[j].reshape(1, HEAD_DIM),
                               lam_init=lam_init)
            b_out = _attention(qkv, mode="moba", n_heads=B_HEADS, q_col=3 * A_HEADS,
                               k_col=3 * A_HEADS + B_HEADS, v_col=3 * A_HEADS + 2 * B_HEADS,
                               slopes=slopes_b)
            parts = [(a_out.reshape(n_tok, a_width), 0), (b_out.reshape(n_tok, b_width), 0)]
            w_out = w_out_even[j]
        else:
            w_in = w_in_odd[j]
            wf = jnp.pad(w_in[:, 3 * c_width:], ((0, 0), (0, LANES - C_HEADS)))
            qkv, z = _inproj(x2, attn_norm_g[l], mod, w_in.astype(BF16), jnp.asarray(cs_odd), wf,
                             seq=seq)
            qkv = qkv.reshape(bsz, seq, -1)
            bf_row = jnp.pad(b_forget[j].astype(F32), (0, LANES - C_HEADS)).reshape(1, LANES)
            cum = _forget_cumsum(z.reshape(bsz, seq, LANES), bf_row)
            tk = ATTN_TILES["fox"][1]
            cum = cum[:, :C_HEADS, :].reshape(bsz, C_HEADS, seq // tk, tk)
            c_out = _attention(qkv, mode="fox", n_heads=C_HEADS, q_col=0, k_col=C_HEADS,
                               v_col=2 * C_HEADS, cum=cum)
            c_out = c_out.reshape(n_tok, c_width)
            parts = [(c_out, 0), (c_out, 1)]
            w_out = w_out_odd[j]
        x2, h2, gates, group_rows = _outproj(parts, w_out.astype(BF16), x2, mod, ffn_norm_g[l],
                                             router_w_pad, router_b_t, seq=seq)
        y = _moe(h2, gates, group_rows, moe_w1[l].astype(BF16), moe_w3[l].astype(BF16),
                 moe_w2[l].astype(BF16))
        x2 = _combine(x2, y, mod, final_norm_g, seq=seq, final_norm=(l == depth - 1))
    return x2.reshape(bsz, seq, d)
```

```python
import functools
import math

import jax
import jax.numpy as jnp
import numpy as np
from jax import lax
from jax.experimental import pallas as pl
from jax.experimental.pallas import tpu as pltpu

F32 = jnp.float32
BF16 = jnp.bfloat16

HEAD_DIM = 128
A_HEADS = 8
A_QK_DIM = 64
B_HEADS = 8
C_HEADS = 16
MOBA_BLOCK = 256
MOBA_TOPK = 3
N_EXPERTS = 16
N_GROUPS = 4
EXPERTS_PER_GROUP = N_EXPERTS // N_GROUPS
EPS = 1e-6
NEG_INF = -1e30
LOG2E = 1.4426950408889634
GROUP_LANE = N_EXPERTS

LANES = 128
SUBLANES = 8
VMEM_LIMIT_BYTES = 56 * 1024 * 1024

NT_DIMS = (((1,), (1,)), ((), ()))


def _dot_bf16x3(a, b):
    a_hi = a.astype(BF16)
    a_lo = (a - a_hi.astype(F32)).astype(BF16)
    b_hi = b.astype(BF16)
    b_lo = (b - b_hi.astype(F32)).astype(BF16)
    return (jnp.dot(a_hi, b_hi, preferred_element_type=F32)
            + jnp.dot(a_hi, b_lo, preferred_element_type=F32)
            + jnp.dot(a_lo, b_hi, preferred_element_type=F32))


def _lane_repeat(x, n):
    return x if n == 1 else jnp.concatenate([x] * n, axis=1)


def _compiler_params(semantics):
    return pltpu.CompilerParams(dimension_semantics=semantics, vmem_limit_bytes=VMEM_LIMIT_BYTES)


def _adaln_kernel(c_ref, w_ref, b_ref, o_ref):
    c = c_ref[...]
    cond = c * jax.nn.sigmoid(c)
    o_ref[0] = _dot_bf16x3(cond, w_ref[0]) + b_ref[0]


def _adaln(c, ada_w, ada_b, *, tn=1024):
    depth, d, n = ada_w.shape
    bsz = c.shape[0]
    rows = -(-bsz // SUBLANES) * SUBLANES
    c_pad = jnp.pad(c, ((0, rows - bsz), (0, 0)))
    out = pl.pallas_call(
        _adaln_kernel,
        out_shape=jax.ShapeDtypeStruct((depth, rows, n), F32),
        grid=(depth, n // tn),
        in_specs=[
            pl.BlockSpec((rows, d), lambda l, j: (0, 0)),
            pl.BlockSpec((1, d, tn), lambda l, j: (l, 0, j)),
            pl.BlockSpec((1, 1, tn), lambda l, j: (l, 0, j)),
        ],
        out_specs=pl.BlockSpec((1, rows, tn), lambda l, j: (l, 0, j)),
        compiler_params=_compiler_params(("arbitrary", "arbitrary")),
        name="adaln",
    )(c_pad, ada_w, ada_b.reshape(depth, 1, n))
    return out[:, :bsz].reshape(depth, bsz, 6, d)


def _modulated_norm(x, gain_row, mod_ref, shift_idx, scale_idx):
    ms = jnp.mean(x * x, axis=-1, keepdims=True)
    xn = x * lax.rsqrt(ms + EPS)
    gain = gain_row * (1.0 + mod_ref[0, scale_idx:scale_idx + 1, :])
    return xn * gain + mod_ref[0, shift_idx:shift_idx + 1, :]


def _inproj_kernel(*refs, with_forget):
    if with_forget:
        x_ref, g_ref, mod_ref, w_ref, cs_ref, wf_ref, o_ref, z_ref, h_sc = refs
    else:
        x_ref, g_ref, mod_ref, w_ref, cs_ref, o_ref, h_sc = refs

    @pl.when(pl.program_id(1) == 0)
    def _():
        h = _modulated_norm(x_ref[...], g_ref[...], mod_ref, 0, 1)
        h_sc[...] = h.astype(BF16)
        if with_forget:
            z_ref[...] = _dot_bf16x3(h, wf_ref[...])

    acc = jnp.dot(h_sc[...], w_ref[...], preferred_element_type=F32)
    o_ref[...] = (acc * cs_ref[...]).astype(o_ref.dtype)


def _inproj(x2, gain, mod, w_bf16, col_scale, wf=None, *, seq, tm=1024, tn=1024):
    n_tok, d = x2.shape
    n_out = col_scale.shape[1]
    tiles_per_seq = seq // tm
    with_forget = wf is not None
    in_specs = [
        pl.BlockSpec((tm, d), lambda i, j: (i, 0)),
        pl.BlockSpec((1, d), lambda i, j: (0, 0)),
        pl.BlockSpec((1, 6, d), lambda i, j: (i // tiles_per_seq, 0, 0)),
        pl.BlockSpec((d, tn), lambda i, j: (0, j)),
        pl.BlockSpec((1, tn), lambda i, j: (0, j)),
    ]
    args = [x2, gain.reshape(1, d), mod, w_bf16, col_scale]
    out_shape = [jax.ShapeDtypeStruct((n_tok, n_out), BF16)]
    out_specs = [pl.BlockSpec((tm, tn), lambda i, j: (i, j))]
    if with_forget:
        in_specs.append(pl.BlockSpec((d, LANES), lambda i, j: (0, 0)))
        args.append(wf)
        out_shape.append(jax.ShapeDtypeStruct((n_tok, LANES), F32))
        out_specs.append(pl.BlockSpec((tm, LANES), lambda i, j: (i, 0)))
    return pl.pallas_call(
        functools.partial(_inproj_kernel, with_forget=with_forget),
        out_shape=out_shape,
        grid=(n_tok // tm, n_out // tn),
        in_specs=in_specs,
        out_specs=out_specs,
        scratch_shapes=[pltpu.VMEM((tm, d), BF16)],
        compiler_params=_compiler_params(("arbitrary", "arbitrary")),
        name="inproj_forget" if with_forget else "inproj",
    )(*args)


def _fcum_kernel(z_ref, bf_ref, o_ref, carry_sc):
    @pl.when(pl.program_id(1) == 0)
    def _():
        carry_sc[...] = jnp.zeros_like(carry_sc)

    z = z_ref[0] + bf_ref[...]
    log_f = jnp.minimum(z, 0.0) - jnp.log1p(jnp.exp(-jnp.abs(z)))
    tc = z.shape[0]
    row = lax.broadcasted_iota(jnp.int32, (tc, tc), 0)
    col = lax.broadcasted_iota(jnp.int32, (tc, tc), 1)
    tri = jnp.where(row >= col, 1.0, 0.0).astype(BF16)
    cum = sum(jnp.dot(tri, part, preferred_element_type=F32)
              for part in _split_bf16x3(log_f)) + carry_sc[...]
    carry_sc[...] = cum[tc - 1:tc, :]
    o_ref[0] = cum.T


def _forget_cumsum(z, b_forget_row, *, tc=256):
    bsz, seq, _ = z.shape
    return pl.pallas_call(
        _fcum_kernel,
        out_shape=jax.ShapeDtypeStruct((bsz, LANES, seq), F32),
        grid=(bsz, seq // tc),
        in_specs=[
            pl.BlockSpec((1, tc, LANES), lambda b, t: (b, t, 0)),
            pl.BlockSpec((1, LANES), lambda b, t: (0, 0)),
        ],
        out_specs=pl.BlockSpec((1, LANES, tc), lambda b, t: (b, 0, t)),
        scratch_shapes=[pltpu.VMEM((1, LANES), F32)],
        compiler_params=_compiler_params(("arbitrary", "arbitrary")),
        name="forget_cumsum",
    )(z, b_forget_row)


def _attn_kernel(*refs, mode, tq, tk, seq, lam_init):
    if mode == "diff":
        slopes_ref, q_ref, k_ref, v_ref, lam_ref, subg_ref, o_ref = refs[:7]
        m_sc, l_sc, acc_sc, sa_sc, sb_sc, vx_sc = refs[7:]
    elif mode == "moba":
        slopes_ref, q_ref, k_ref, v_ref, o_ref = refs[:5]
        m_sc, l_sc, acc_sc, sa_sc, sb_sc, vx_sc, kmean_sc, sel_sc, kx_sc = refs[5:]
    else:
        q_ref, k_ref, v_ref, cum_ref, o_ref = refs[:5]
        m_sc, l_sc, acc_sc, sa_sc, sb_sc, vx_sc = refs[5:]

    head = pl.program_id(1)
    qi = pl.program_id(2)
    q0 = qi * tq
    rows = 2 * tq if mode == "diff" else tq
    reps = tk // LANES
    n_diag = tq // tk

    q = q_ref[0]
    if mode == "diff":
        lane = lax.broadcasted_iota(jnp.int32, (tq, HEAD_DIM), 1)
        zero = jnp.zeros_like(q)
        q = jnp.concatenate([jnp.where(lane < A_QK_DIM, q, zero),
                             jnp.where(lane >= A_QK_DIM, q, zero)], axis=0)

    def col_bias(j):
        if mode == "fox":
            return cum_ref[0, 0, pl.ds(j, 1), :] * (-LOG2E)
        kpos = lax.broadcasted_iota(jnp.int32, (1, tk), 1) + (j * tk - q0)
        return kpos.astype(F32) * slopes_ref[head]

    def scores(j, r0=0):
        start = pl.multiple_of(j * tk, tk)
        k = k_ref[0, pl.ds(start, tk), :]
        s = lax.dot_general(q[r0:], k, NT_DIMS, preferred_element_type=F32)
        return s + col_bias(j), start

    def chosen_bias(j, r0=0):
        jf = j.astype(F32)
        return jnp.where(sel_sc[0, r0:, :] == jf, 0.0,
                         jnp.where(sel_sc[1, r0:, :] == jf, 0.0,
                                   jnp.where(sel_sc[2, r0:, :] == jf, 0.0, NEG_INF)))

    if mode == "moba":
        n_blocks = seq // MOBA_BLOCK
        sub_per_tile = tq // MOBA_BLOCK

        @pl.when(qi == 0)
        def _():
            kmean_sc[...] = jnp.zeros_like(kmean_sc)
            for n in range(n_blocks):
                blk = k_ref[0, n * MOBA_BLOCK:(n + 1) * MOBA_BLOCK, :].astype(F32)
                kmean_sc[n:n + 1, :] = jnp.mean(blk, axis=0, keepdims=True)

        gate = sum(lax.dot_general(q, part, NT_DIMS, preferred_element_type=F32)
                   for part in _split_bf16x3(kmean_sc[...]))
        blk_id = lax.broadcasted_iota(jnp.int32, (tq, LANES), 1).astype(F32)
        row_sub = lax.broadcasted_iota(jnp.int32, (tq, LANES), 0) // MOBA_BLOCK
        own = (row_sub + qi * sub_per_tile).astype(F32)
        gate = jnp.where(blk_id < own, gate, NEG_INF)
        block_bias = jnp.full((tq, LANES), NEG_INF, F32)
        for r in range(MOBA_TOPK):
            best = jnp.max(gate, axis=1, keepdims=True)
            pick = jnp.min(jnp.where(gate == best, blk_id, float(LANES)), axis=1, keepdims=True)
            pick = jnp.where(pick < own, pick, -1.0)
            sel_sc[r] = jnp.broadcast_to(pick, (tq, LANES))
            block_bias = jnp.where(blk_id == pick, 0.0, block_bias)
            gate = jnp.where(blk_id == pick, -jnp.inf, gate)

        slope = slopes_ref[head]

        @pl.when(qi == 0)
        def _():
            kx_sc[:, 0:HEAD_DIM] = k_ref[0]
            lane_k = lax.broadcasted_iota(jnp.int32, (seq, LANES), 1)
            kpos = lax.broadcasted_iota(jnp.int32, (seq, LANES), 0)
            hi, mid, lo = _split_bf16x3(kpos.astype(F32) * slope)
            aux = jnp.where(lane_k == kpos // MOBA_BLOCK, 1.0, 0.0)
            aux = jnp.where(lane_k == n_blocks, hi.astype(F32), aux)
            aux = jnp.where(lane_k == n_blocks + 1, mid.astype(F32), aux)
            aux = jnp.where(lane_k == n_blocks + 2, lo.astype(F32), aux)
            aux = jnp.where((lane_k >= n_blocks + 3) & (lane_k < n_blocks + 6), 1.0, aux)
            kx_sc[:, HEAD_DIM:2 * HEAD_DIM] = aux.astype(BF16)

        lane_q = lax.broadcasted_iota(jnp.int32, (tq, LANES), 1)
        off = jnp.full((tq, LANES), 0.0, F32) - slope * q0.astype(F32)
        hi, mid, lo = _split_bf16x3(off)
        aux_q = jnp.where(lane_q < n_blocks, block_bias, 0.0)
        aux_q = jnp.where((lane_q >= n_blocks) & (lane_q < n_blocks + 3), 1.0, aux_q)
        aux_q = jnp.where(lane_q == n_blocks + 3, hi.astype(F32), aux_q)
        aux_q = jnp.where(lane_q == n_blocks + 4, mid.astype(F32), aux_q)
        aux_q = jnp.where(lane_q == n_blocks + 5, lo.astype(F32), aux_q)
        q_wide = jnp.concatenate([q, aux_q.astype(BF16)], axis=1)

    def past_scores(j):
        if mode != "moba":
            return scores(j)[0]
        start = pl.multiple_of(j * tk, tk)
        return lax.dot_general(q_wide, kx_sc[pl.ds(start, tk), :], NT_DIMS,
                               preferred_element_type=F32)

    @pl.when(qi == 0)
    def _():
        vx_sc[:, 0:HEAD_DIM] = v_ref[0]
        vx_sc[:, HEAD_DIM:2 * HEAD_DIM] = jnp.ones((seq, LANES), BF16)

    def accumulate(s, start, first, r0=0):
        v = vx_sc[pl.ds(start, tk), :]
        if first:
            m_new = jnp.max(s, axis=1, keepdims=True)
            p = jnp.exp2(s - m_new)
            pv = jnp.dot(p.astype(BF16), v, preferred_element_type=F32)
            m_sc[...] = jnp.broadcast_to(m_new, (rows, LANES))
            l_sc[...] = pv[:, HEAD_DIM:2 * HEAD_DIM]
            acc_sc[...] = pv[:, 0:HEAD_DIM]
            return
        m_prev = m_sc[r0:, :]
        m_new = jnp.maximum(m_prev, jnp.max(s, axis=1, keepdims=True))
        p = jnp.exp2(s - _lane_repeat(m_new, reps))
        alpha = jnp.exp2(m_prev - m_new)
        pv = jnp.dot(p.astype(BF16), v, preferred_element_type=F32)
        l_sc[r0:, :] = alpha * l_sc[r0:, :] + pv[:, HEAD_DIM:2 * HEAD_DIM]
        acc_sc[r0:, :] = alpha * acc_sc[r0:, :] + pv[:, 0:HEAD_DIM]
        m_sc[r0:, :] = m_new

    jd = qi * n_diag
    for c in range(n_diag):
        r0 = 0 if mode == "diff" else c * tk
        s, start = scores(jd + c, r0)
        rpos = lax.broadcasted_iota(jnp.int32, (rows - r0, tk), 0) + r0
        if mode == "diff":
            rpos = jnp.where(rpos >= tq, rpos - tq, rpos)
        cpos = lax.broadcasted_iota(jnp.int32, (rows - r0, tk), 1) + c * tk
        causal = cpos <= rpos
        if mode == "moba":
            s_sel = s + _lane_repeat(chosen_bias(jd + c, r0), reps)
            s = jnp.where(rpos < (c + 1) * MOBA_BLOCK, jnp.where(causal, s, NEG_INF), s_sel)
        else:
            s = jnp.where(causal, s, NEG_INF)
        accumulate(s, start, first=(c == 0), r0=r0)

    def consume(s, j):
        accumulate(s, pl.multiple_of(j * tk, tk), first=False)

    def past_pair(jj, carry):
        j0 = 2 * jj
        sb_sc[...] = past_scores(j0 + 1)
        consume(sa_sc[...], j0)
        sa_sc[...] = past_scores(jnp.minimum(j0 + 2, jd - 2))
        consume(sb_sc[...], j0 + 1)
        return carry

    sa_sc[...] = past_scores(0)
    lax.fori_loop(0, qi * (n_diag // 2), past_pair, 0)

    out = acc_sc[...] / l_sc[...]
    if mode == "diff":
        lam_rows = lam_ref[...]
        lam = (jnp.exp(jnp.sum(lam_rows[0:1] * lam_rows[1:2], axis=1, keepdims=True))
               - jnp.exp(jnp.sum(lam_rows[2:3] * lam_rows[3:4], axis=1, keepdims=True)) + lam_init)
        a = out[:tq] - lam * out[tq:]
        ms = jnp.mean(a * a, axis=-1, keepdims=True)
        out = a * lax.rsqrt(ms + EPS) * subg_ref[...] * (1.0 - lam_init)
    o_ref[0] = out.astype(o_ref.dtype)


ATTN_TILES = {"diff": (1024, 512), "moba": (1024, MOBA_BLOCK), "fox": (1024, 512)}


def _attention(qkv, *, mode, n_heads, q_col, k_col, v_col, slopes=None, lam_rows=None,
               subln_g=None, cum=None, lam_init=0.0):
    bsz, seq, _ = qkv.shape
    tq, tk = ATTN_TILES[mode]
    tq = min(tq, seq)
    assert tq % (2 * tk) == 0 and seq % tq == 0
    rows = 2 * tq if mode == "diff" else tq
    q_spec = pl.BlockSpec((1, tq, HEAD_DIM), lambda b, h, i: (b, i, q_col + h))
    k_spec = pl.BlockSpec((1, seq, HEAD_DIM), lambda b, h, i: (b, 0, k_col + h))
    v_spec = pl.BlockSpec((1, seq, HEAD_DIM), lambda b, h, i: (b, 0, v_col + h))
    smem = pl.BlockSpec(memory_space=pltpu.SMEM)
    scratch = [pltpu.VMEM((rows, LANES), F32), pltpu.VMEM((rows, LANES), F32),
               pltpu.VMEM((rows, HEAD_DIM), F32),
               pltpu.VMEM((rows, tk), F32), pltpu.VMEM((rows, tk), F32),
               pltpu.VMEM((seq, 2 * HEAD_DIM), BF16)]
    if mode == "diff":
        in_specs = [smem, q_spec, k_spec, v_spec,
                    pl.BlockSpec((SUBLANES, LANES), lambda b, h, i: (0, 0)),
                    pl.BlockSpec((1, HEAD_DIM), lambda b, h, i: (0, 0))]
        args = [slopes, qkv, qkv, qkv, lam_rows, subln_g]
    elif mode == "moba":
        assert tk == MOBA_BLOCK and seq // MOBA_BLOCK <= LANES
        in_specs = [smem, q_spec, k_spec, v_spec]
        args = [slopes, qkv, qkv, qkv]
        scratch += [pltpu.VMEM((LANES, HEAD_DIM), F32), pltpu.VMEM((MOBA_TOPK, tq, LANES), F32),
                    pltpu.VMEM((seq, 2 * HEAD_DIM), BF16)]
    else:
        in_specs = [q_spec, k_spec, v_spec,
                    pl.BlockSpec((1, 1, seq // tk, tk), lambda b, h, i: (b, h, 0, 0))]
        args = [qkv, qkv, qkv, cum]
    return pl.pallas_call(
        functools.partial(_attn_kernel, mode=mode, tq=tq, tk=tk, seq=seq, lam_init=lam_init),
        out_shape=jax.ShapeDtypeStruct((bsz, seq, n_heads * HEAD_DIM), BF16),
        grid=(bsz, n_heads, seq // tq),
        in_specs=in_specs,
        out_specs=pl.BlockSpec((1, tq, HEAD_DIM), lambda b, h, i: (b, i, h)),
        scratch_shapes=scratch,
        compiler_params=_compiler_params(("arbitrary", "arbitrary", "arbitrary")),
        name="attn_" + mode,
    )(*args)


def _route(logits_t, bias_t):
    aff = jax.nn.sigmoid(logits_t)
    sel = aff + bias_t
    sel_rows = [sel[e:e + 1, :] for e in range(N_EXPERTS)]
    aff_rows = [aff[e:e + 1, :] for e in range(N_EXPERTS)]

    def first_argmax(vals):
        idx = jnp.zeros_like(vals[0])
        top = vals[0]
        for r in range(1, len(vals)):
            better = vals[r] > top
            idx = jnp.where(better, float(r), idx)
            top = jnp.where(better, vals[r], top)
        return idx

    group_scores = []
    for g in range(N_GROUPS):
        a, b, c, d = sel_rows[EXPERTS_PER_GROUP * g:EXPERTS_PER_GROUP * (g + 1)]
        group_scores.append(jnp.maximum(jnp.maximum(jnp.maximum(a + b, a + c), jnp.maximum(a + d, b + c)),
                                        jnp.maximum(b + d, c + d)))
    best = first_argmax(group_scores)

    def in_best_group(rows_, r):
        out = rows_[(N_GROUPS - 1) * EXPERTS_PER_GROUP + r]
        for g in range(N_GROUPS - 2, -1, -1):
            out = jnp.where(best == float(g), rows_[g * EXPERTS_PER_GROUP + r], out)
        return out

    cand_sel = [in_best_group(sel_rows, r) for r in range(EXPERTS_PER_GROUP)]
    cand_aff = [in_best_group(aff_rows, r) for r in range(EXPERTS_PER_GROUP)]
    i1 = first_argmax(cand_sel)
    i2 = first_argmax([jnp.where(i1 == float(r), -jnp.inf, cand_sel[r])
                       for r in range(EXPERTS_PER_GROUP)])

    def pick(idx):
        out = cand_aff[EXPERTS_PER_GROUP - 1]
        for r in range(EXPERTS_PER_GROUP - 2, -1, -1):
            out = jnp.where(idx == float(r), cand_aff[r], out)
        return out

    w1, w2 = pick(i1), pick(i2)
    den = w1 + w2
    w1, w2 = w1 / den, w2 / den
    gate_rows = []
    for e in range(N_EXPERTS):
        g, r = divmod(e, EXPERTS_PER_GROUP)
        in_expert = jnp.where(i1 == float(r), w1, jnp.where(i2 == float(r), w2, 0.0))
        gate_rows.append(jnp.where(best == float(g), in_expert, 0.0))
    return jnp.concatenate(gate_rows, axis=0), best


def _outproj_kernel(a_ref, b_ref, w_ref, x_ref, mod_ref, g_ref, rw_ref, rb_ref,
                    xo_ref, h_ref, gates_ref, group_ref):
    half = a_ref.shape[1]
    y = (jnp.dot(a_ref[...], w_ref[0:half, :], preferred_element_type=F32)
         + jnp.dot(b_ref[...], w_ref[half:2 * half, :], preferred_element_type=F32))
    x_new = x_ref[...] + mod_ref[0, 2:3, :] * y
    xo_ref[...] = x_new
    h = _modulated_norm(x_new, g_ref[...], mod_ref, 3, 4)
    h_ref[...] = h.astype(BF16)
    logits = _dot_bf16x3(h, rw_ref[...])
    gates_t, best = _route(logits.T[0:N_EXPERTS, :], rb_ref[...])
    tm = gates_t.shape[1]
    padded = jnp.concatenate(
        [gates_t, best, jnp.zeros((LANES - N_EXPERTS - 1, tm), F32)], axis=0)
    gates_ref[...] = padded.T
    group_ref[0] = best


def _outproj(parts, w_bf16, x2, mod, ffn_gain, router_w_pad, router_b_t, *, seq, tm=512):
    n_tok, d = x2.shape
    half = d // 2
    tiles_per_seq = seq // tm
    (a, ca), (b, cb) = parts
    return pl.pallas_call(
        _outproj_kernel,
        out_shape=[jax.ShapeDtypeStruct((n_tok, d), F32),
                   jax.ShapeDtypeStruct((n_tok, d), BF16),
                   jax.ShapeDtypeStruct((n_tok, LANES), F32),
                   jax.ShapeDtypeStruct((n_tok // tm, 1, tm), F32)],
        grid=(n_tok // tm,),
        in_specs=[
            pl.BlockSpec((tm, half), lambda i: (i, ca)),
            pl.BlockSpec((tm, half), lambda i: (i, cb)),
            pl.BlockSpec((d, d), lambda i: (0, 0)),
            pl.BlockSpec((tm, d), lambda i: (i, 0)),
            pl.BlockSpec((1, 6, d), lambda i: (i // tiles_per_seq, 0, 0)),
            pl.BlockSpec((1, d), lambda i: (0, 0)),
            pl.BlockSpec((d, LANES), lambda i: (0, 0)),
            pl.BlockSpec((N_EXPERTS, tm), lambda i: (0, 0)),
        ],
        out_specs=[pl.BlockSpec((tm, d), lambda i: (i, 0)),
                   pl.BlockSpec((tm, d), lambda i: (i, 0)),
                   pl.BlockSpec((tm, LANES), lambda i: (i, 0)),
                   pl.BlockSpec((1, 1, tm), lambda i: (i, 0, 0))],
        compiler_params=_compiler_params(("arbitrary",)),
        name="outproj_router",
    )(a, b, w_bf16, x2, mod, ffn_gain.reshape(1, d), router_w_pad,
      jnp.broadcast_to(router_b_t, (N_EXPERTS, tm)))


MOE_CHUNK = 256
MOE_TAIL_CHUNK = 128


def _split_bf16x3(x):
    hi = x.astype(BF16)
    r1 = x - hi.astype(F32)
    mid = r1.astype(BF16)
    lo = (r1 - mid.astype(F32)).astype(BF16)
    return hi, mid, lo


def _moe_kernel(h_ref, gates_ref, group_ref, w1_ref, w3_ref, w2_ref, o_ref, acc_sc):
    g = pl.program_id(0)
    gf = g.astype(F32)
    tm = h_ref.shape[0]
    n_local = w1_ref.shape[0]

    gates = gates_ref[...]
    lane = lax.broadcasted_iota(jnp.int32, (tm, LANES), 1)
    grp_col = jnp.sum(jnp.where(lane == GROUP_LANE, gates, 0.0), axis=1, keepdims=True)
    member_col = jnp.broadcast_to(jnp.where(grp_col == gf, 1.0, 0.0), (tm, LANES))
    member_row = jnp.broadcast_to(jnp.where(group_ref[0] == gf, 1.0, 0.0), (SUBLANES, tm))

    ri = lax.broadcasted_iota(jnp.int32, (tm, tm), 0)
    ci = lax.broadcasted_iota(jnp.int32, (tm, tm), 1)
    lower = jnp.where(ci < ri, 1.0, 0.0).astype(BF16)
    upper = jnp.where(ri < ci, 1.0, 0.0).astype(BF16)
    rank_col = jnp.dot(lower, member_col.astype(BF16), preferred_element_type=F32)
    rank_row = jnp.dot(member_row.astype(BF16), upper, preferred_element_type=F32)[0:1, :]
    pos_col = jnp.where(member_col > 0.0, rank_col, -1.0)
    pos_row = jnp.where(member_row[0:1, :] > 0.0, rank_row, -1.0)
    count = jnp.sum(member_row[0:1, :]).astype(jnp.int32)
    n_tail = lax.shift_right_logical(jnp.maximum(count - MOE_CHUNK, 0) + (MOE_TAIL_CHUNK - 1),
                                     MOE_TAIL_CHUNK.bit_length() - 1)

    gates3 = jnp.concatenate(_split_bf16x3(gates), axis=1)
    acc_sc[...] = jnp.zeros_like(acc_sc)

    def run_chunk(size, base):
        slot_r = lax.broadcasted_iota(jnp.int32, (size, tm), 0).astype(F32) + base
        slot_c = lax.broadcasted_iota(jnp.int32, (tm, size), 1).astype(F32) + base
        gather = jnp.where(pos_row == slot_r, 1.0, 0.0).astype(BF16)
        scatter = jnp.where(_lane_repeat(pos_col, size // LANES) == slot_c,
                            1.0, 0.0).astype(BF16)
        hc = jnp.dot(gather, h_ref[...], preferred_element_type=F32).astype(BF16)
        g3 = jnp.dot(gather, gates3, preferred_element_type=F32)
        gc = g3[:, 0:LANES] + g3[:, LANES:2 * LANES] + g3[:, 2 * LANES:3 * LANES]
        clane = lax.broadcasted_iota(jnp.int32, (size, LANES), 1)
        y = None
        for el in range(n_local):
            gate_col = jnp.sum(jnp.where(clane == g * n_local + el, gc, 0.0), axis=1, keepdims=True)
            u = jnp.dot(hc, w1_ref[el], preferred_element_type=F32)
            v = jnp.dot(hc, w3_ref[el], preferred_element_type=F32)
            act = ((u * jax.nn.sigmoid(u)) * v * gate_col).astype(BF16)
            part = jnp.dot(act, w2_ref[el], preferred_element_type=F32)
            y = part if y is None else y + part
        acc_sc[...] += jnp.dot(scatter, y.astype(BF16), preferred_element_type=F32)

    @pl.when(count > 0)
    def _():
        run_chunk(MOE_CHUNK, 0.0)

    def tail_chunk(k, carry):
        run_chunk(MOE_TAIL_CHUNK, (MOE_CHUNK + k * MOE_TAIL_CHUNK).astype(F32))
        return carry

    lax.fori_loop(0, n_tail, tail_chunk, 0)
    o_ref[0] = acc_sc[...].astype(o_ref.dtype)


def _moe(h2, gates, group_rows, w1, w3, w2, *, tm=1024):
    n_tok, d = h2.shape
    n_exp, _, d_ff = w1.shape
    tm = min(tm, n_tok)
    per_group = n_exp // N_GROUPS
    once = pl.Buffered(1)
    return pl.pallas_call(
        _moe_kernel,
        out_shape=jax.ShapeDtypeStruct((N_GROUPS, n_tok, d), BF16),
        grid=(N_GROUPS, n_tok // tm),
        in_specs=[
            pl.BlockSpec((tm, d), lambda g, i: (i, 0)),
            pl.BlockSpec((tm, LANES), lambda g, i: (i, 0)),
            pl.BlockSpec((1, 1, tm), lambda g, i: (i, 0, 0)),
            pl.BlockSpec((per_group, d, d_ff), lambda g, i: (g, 0, 0), pipeline_mode=once),
            pl.BlockSpec((per_group, d, d_ff), lambda g, i: (g, 0, 0), pipeline_mode=once),
            pl.BlockSpec((per_group, d_ff, d), lambda g, i: (g, 0, 0), pipeline_mode=once),
        ],
        out_specs=pl.BlockSpec((1, tm, d), lambda g, i: (g, i, 0)),
        scratch_shapes=[pltpu.VMEM((tm, d), F32)],
        compiler_params=_compiler_params(("arbitrary", "arbitrary")),
        name="moe_grouped",
    )(h2, gates, group_rows.reshape(n_tok // tm, 1, tm), w1, w3, w2)


def _combine_kernel(x_ref, y_ref, mod_ref, fg_ref, o_ref, *, final_norm):
    y = y_ref[0].astype(F32)
    for g in range(1, y_ref.shape[0]):
        y = y + y_ref[g].astype(F32)
    x_new = x_ref[...] + mod_ref[0, 5:6, :] * y
    if final_norm:
        ms = jnp.mean(x_new * x_new, axis=-1, keepdims=True)
        x_new = x_new * lax.rsqrt(ms + EPS) * fg_ref[...]
    o_ref[...] = x_new


def _combine(x2, y, mod, final_gain, *, seq, final_norm, tm=512):
    n_tok, d = x2.shape
    tm = min(tm, seq)
    tiles_per_seq = seq // tm
    return pl.pallas_call(
        functools.partial(_combine_kernel, final_norm=final_norm),
        out_shape=jax.ShapeDtypeStruct((n_tok, d), F32),
        grid=(n_tok // tm,),
        in_specs=[
            pl.BlockSpec((tm, d), lambda i: (i, 0)),
            pl.BlockSpec((y.shape[0], tm, d), lambda i: (0, i, 0)),
            pl.BlockSpec((1, 6, d), lambda i: (i // tiles_per_seq, 0, 0)),
            pl.BlockSpec((1, d), lambda i: (0, 0)),
        ],
        out_specs=pl.BlockSpec((tm, d), lambda i: (i, 0)),
        compiler_params=_compiler_params(("arbitrary",)),
        name="combine_final" if final_norm else "combine",
    )(x2, y, mod, final_gain.reshape(1, d))


def _alibi_slopes(n):
    return 2.0 ** (-8.0 * np.arange(1, n + 1) / n)


def kernel(x, c, ada_w, ada_b, attn_norm_g, ffn_norm_g, w_in_even, w_out_even, lam_q1, lam_k1,
           lam_q2, lam_k2, subln_g, w_in_odd, b_forget, w_out_odd, router_w, router_b, moe_w1,
           moe_w3, moe_w2, final_norm_g):
    bsz, seq, d = x.shape
    depth = ada_w.shape[0]
    n_tok = bsz * seq
    x2 = x.reshape(n_tok, d)

    mod_all = _adaln(c, ada_w, ada_b)

    slopes = _alibi_slopes(A_HEADS + B_HEADS)
    slopes_a = jnp.asarray(slopes[0::2] * LOG2E, F32)
    slopes_b = jnp.asarray(slopes[1::2] * LOG2E, F32)
    a_width = A_HEADS * HEAD_DIM
    b_width = B_HEADS * HEAD_DIM
    c_width = C_HEADS * HEAD_DIM
    cs_even = np.ones((1, 3 * a_width + 3 * b_width), np.float32)
    cs_even[0, :a_width] = A_QK_DIM ** -0.5 * LOG2E
    cs_even[0, 3 * a_width:3 * a_width + b_width] = HEAD_DIM ** -0.5 * LOG2E
    cs_odd = np.ones((1, 3 * c_width), np.float32)
    cs_odd[0, :c_width] = HEAD_DIM ** -0.5 * LOG2E

    router_w_pad = jnp.pad(router_w, ((0, 0), (0, LANES - N_EXPERTS)))
    router_b_t = router_b.reshape(N_EXPERTS, 1)

    for l in range(depth):
        mod = mod_all[l]
        j = l // 2
        if l % 2 == 0:
            (qkv,) = _inproj(x2, attn_norm_g[l], mod, w_in_even[j].astype(BF16),
                             jnp.asarray(cs_even), seq=seq)
            qkv = qkv.reshape(bsz, seq, -1)
            lam_init = 0.8 - 0.6 * math.exp(-0.3 * l)
            lam_rows = jnp.zeros((SUBLANES, LANES), F32)
            lam_rows = lam_rows.at[0:4, 0:A_QK_DIM].set(
                jnp.stack([lam_q1[j], lam_k1[j], lam_q2[j], lam_k2[j]]).astype(F32))
            a_out = _attention(qkv, mode="diff", n_heads=A_HEADS, q_col=0, k_col=A_HEADS,
                               v_col=2 * A_HEADS, slopes=slopes_a,
                               lam_rows=lam_rows, subln_g=subln_g[j].reshape(1, HEAD_DIM),
                               lam_init=lam_init)
            b_out = _attention(qkv, mode="moba", n_heads=B_HEADS, q_col=3 * A_HEADS,
                               k_col=3 * A_HEADS + B_HEADS, v_col=3 * A_HEADS + 2 * B_HEADS,
                               slopes=slopes_b)
            parts = [(a_out.reshape(n_tok, a_width), 0), (b_out.reshape(n_tok, b_width), 0)]
            w_out = w_out_even[j]
        else:
            w_in = w_in_odd[j]
            wf = jnp.pad(w_in[:, 3 * c_width:], ((0, 0), (0, LANES - C_HEADS)))
            qkv, z = _inproj(x2, attn_norm_g[l], mod, w_in.astype(BF16), jnp.asarray(cs_odd), wf,
                             seq=seq)
            qkv = qkv.reshape(bsz, seq, -1)
            bf_row = jnp.pad(b_forget[j].astype(F32), (0, LANES - C_HEADS)).reshape(1, LANES)
            cum = _forget_cumsum(z.reshape(bsz, seq, LANES), bf_row)
            tk = ATTN_TILES["fox"][1]
            cum = cum[:, :C_HEADS, :].reshape(bsz, C_HEADS, seq // tk, tk)
            c_out = _attention(qkv, mode="fox", n_heads=C_HEADS, q_col=0, k_col=C_HEADS,
                               v_col=2 * C_HEADS, cum=cum)
            c_out = c_out.reshape(n_tok, c_width)
            parts = [(c_out, 0), (c_out, 1)]
            w_out = w_out_odd[j]
        x2, h2, gates, group_rows = _outproj(parts, w_out.astype(BF16), x2, mod, ffn_norm_g[l],
                                             router_w_pad, router_b_t, seq=seq)
        y = _moe(h2, gates, group_rows, moe_w1[l].astype(BF16), moe_w3[l].astype(BF16),
                 moe_w2[l].astype(BF16))
        x2 = _combine(x2, y, mod, final_norm_g, seq=seq, final_norm=(l == depth - 1))
    return x2.reshape(bsz, seq, d)
```
